```python
import math
import jax
import jax.numpy as jnp
from jax import lax
import numpy as np

D_MODEL = 2048
BATCH = 8
SEQ = 4096
DEPTH = 2

D_RET = 3 * D_MODEL // 8
D_MLSTM = 3 * D_MODEL // 8
D_S5 = D_MODEL // 4
D_MIX = D_RET + D_MLSTM + D_S5
RET_V_DIM = 128
RET_QK_DIM = 64
RET_HEADS = D_RET // RET_V_DIM
MLSTM_DIM = 128
MLSTM_HEADS = D_MLSTM // MLSTM_DIM
MLSTM_CONV = 4
S5_GROUP_CH = 16
S5_GROUPS = D_S5 // S5_GROUP_CH
S5_STATE = 64
D_FF = 5632
FFN_CONV = 3
CHUNK = 128
ROPE_BASE = 10000.0
EPS = 1e-6
D_IN = 2 * RET_HEADS * RET_QK_DIM + 2 * D_RET + 4 * D_MLSTM + 2 * MLSTM_HEADS + D_S5

kernel_name = "hybrid_retention_mlstm_s5_convffn"


def rmsnorm(x, w):
    xf = x.astype(jnp.float32)
    y = xf * lax.rsqrt(jnp.mean(xf * xf, axis=-1, keepdims=True) + EPS)
    return (y * w.astype(jnp.float32)).astype(x.dtype)


def head_groupnorm(h, w):
    b, t, nh, dh = h.shape
    hf = h.astype(jnp.float32)
    mu = jnp.mean(hf, axis=-1, keepdims=True)
    var = jnp.mean(jnp.square(hf - mu), axis=-1, keepdims=True)
    y = ((hf - mu) * lax.rsqrt(var + EPS)).reshape(b, t, nh * dh)
    return (y * w.astype(jnp.float32)).astype(h.dtype)


def causal_dwconv(x, w, b):
    k = w.shape[0]
    y = lax.conv_general_dilated(x, w[:, None, :], window_strides=(1,), padding=[(k - 1, 0)],
                                 dimension_numbers=("NWC", "WIO", "NWC"),
                                 feature_group_count=x.shape[-1])
    return y + b


def rotary(x):
    t, dh = x.shape[1], x.shape[-1]
    inv = ROPE_BASE ** (-jnp.arange(0, dh, 2, dtype=jnp.float32) / dh)
    ang = jnp.arange(t, dtype=jnp.float32)[:, None] * inv[None, :]
    cos = jnp.cos(ang)[None, :, None, :].astype(x.dtype)
    sin = jnp.sin(ang)[None, :, None, :].astype(x.dtype)
    x1, x2 = jnp.split(x, 2, axis=-1)
    return jnp.concatenate([x1 * cos - x2 * sin, x1 * sin + x2 * cos], axis=-1)


def to_chunks(a):
    b, t, nh, d = a.shape
    return a.reshape(b, t // CHUNK, CHUNK, nh, d).transpose(1, 0, 3, 2, 4)


def from_chunks(a):
    n, b, nh, l, d = a.shape
    return a.transpose(1, 0, 3, 2, 4).reshape(b, n * l, nh, d)


def retention(q, k, v):
    b, t, nh, dk = q.shape
    dv = v.shape[-1]
    dt = q.dtype
    log_gamma = jnp.log1p(-jnp.exp2(-5.0 - jnp.arange(nh, dtype=jnp.float32)))
    idx = jnp.arange(CHUNK, dtype=jnp.float32)
    diff = idx[:, None] - idx[None, :]
    intra = jnp.where(diff[None] >= 0, jnp.exp(jnp.maximum(diff, 0.0)[None] * log_gamma[:, None, None]), 0.0).astype(dt)
    q_decay = jnp.exp((idx[None, :] + 1.0) * log_gamma[:, None])[None, :, :, None].astype(dt)
    k_decay = jnp.exp((CHUNK - 1.0 - idx)[None, :] * log_gamma[:, None])[None, :, :, None].astype(dt)
    chunk_decay = jnp.exp(CHUNK * log_gamma)[None, :, None, None].astype(dt)
    q = q * (dk ** -0.5)

    def step(state, inp):
        qi, ki, vi = inp
        s = jnp.einsum('bhld,bhmd->bhlm', qi, ki) * intra
        inner = jnp.einsum('bhlm,bhmv->bhlv', s, vi)
        cross = jnp.einsum('bhld,bhdv->bhlv', qi, state) * q_decay
        state = state * chunk_decay + jnp.einsum('bhld,bhlv->bhdv', ki * k_decay, vi)
        return state, inner + cross

    state0 = jnp.zeros((b, nh, dk, dv), dt)
    _, out = lax.scan(step, state0, (to_chunks(q), to_chunks(k), to_chunks(v)))
    return from_chunks(out)


def mlstm(q, k, v, i_pre, f_pre):
    b, t, nh, d = q.shape
    f32 = jnp.float32
    qf = q.astype(f32)
    kf = k.astype(f32) * (d ** -0.5)
    vf = v.astype(f32)
    ig = i_pre.astype(f32)
    logf = jax.nn.log_sigmoid(f_pre.astype(f32))
    gate_chunks = lambda a: a.reshape(b, t // CHUNK, CHUNK, nh).transpose(1, 0, 3, 2)
    causal = jnp.tril(jnp.ones((CHUNK, CHUNK), dtype=bool))

    def step(carry, inp):
        c_st, n_st, m_st = carry
        qi, ki, vi, ii, lf = inp
        cum = jnp.cumsum(lf, axis=-1)
        logw = jnp.where(causal, cum[..., :, None] - cum[..., None, :] + ii[..., None, :], -jnp.inf)
        inter = cum + m_st[..., None]
        m_t = jnp.maximum(inter, jnp.max(logw, axis=-1))
        w = jnp.exp(logw - m_t[..., None])
        sc = jnp.exp(inter - m_t)
        qk = jnp.einsum('bhld,bhsd->bhls', qi, ki) * w
        num = jnp.einsum('bhls,bhsd->bhld', qk, vi) + sc[..., None] * jnp.einsum('bhld,bhde->bhle', qi, c_st)
        den = jnp.sum(qk, axis=-1) + sc * jnp.einsum('bhld,bhd->bhl', qi, n_st)
        h = num / jnp.maximum(jnp.abs(den), jnp.exp(-m_t))[..., None]
        last = cum[..., -1]
        logw_end = last[..., None] - cum + ii
        m_new = jnp.maximum(last + m_st, jnp.max(logw_end, axis=-1))
        kw = ki * jnp.exp(logw_end - m_new[..., None])[..., None]
        decay = jnp.exp(last + m_st - m_new)
        c_new = decay[..., None, None] * c_st + jnp.einsum('bhsd,bhse->bhde', kw, vi)
        n_new = decay[..., None] * n_st + jnp.sum(kw, axis=-2)
        return (c_new, n_new, m_new), h

    carry0 = (jnp.zeros((b, nh, d, d), f32), jnp.zeros((b, nh, d), f32), jnp.zeros((b, nh), f32))
    _, hs = lax.scan(step, carry0, (to_chunks(qf), to_chunks(kf), to_chunks(vf), gate_chunks(ig), gate_chunks(logf)))
    return from_chunks(hs).astype(q.dtype)


def s5(u, a_re, a_im, log_step, b_re, b_im, c_re, c_im, d_skip, glu_w, glu_b):
    bsz, t, _ = u.shape
    f32 = jnp.float32
    uf = u.astype(f32).reshape(bsz, t, S5_GROUPS, S5_GROUP_CH)
    ar, ai = a_re.astype(f32), a_im.astype(f32)
    step = jnp.exp(log_step.astype(f32))[:, None]
    mag = jnp.exp(ar * step)
    abar_re, abar_im = mag * jnp.cos(ai * step), mag * jnp.sin(ai * step)
    xr, xi = abar_re - 1.0, abar_im
    den = ar * ar + ai * ai
    fr, fi = (xr * ar + xi * ai) / den, (xi * ar - xr * ai) / den
    br, bi = b_re.astype(f32), b_im.astype(f32)
    bbar_re = fr[..., None] * br - fi[..., None] * bi
    bbar_im = fr[..., None] * bi + fi[..., None] * br
    bu_re = jnp.einsum('btgc,gpc->btgp', uf, bbar_re)
    bu_im = jnp.einsum('btgc,gpc->btgp', uf, bbar_im)
    at_re = jnp.broadcast_to(abar_re, (1, t, S5_GROUPS, S5_STATE))
    at_im = jnp.broadcast_to(abar_im, (1, t, S5_GROUPS, S5_STATE))

    def combine(e1, e2):
        a1r, a1i, b1r, b1i = e1
        a2r, a2i, b2r, b2i = e2
        return (a2r * a1r - a2i * a1i, a2r * a1i + a2i * a1r,
                a2r * b1r - a2i * b1i + b2r, a2r * b1i + a2i * b1r + b2i)

    _, _, s_re, s_im = lax.associative_scan(combine, (at_re, at_im, bu_re, bu_im), axis=1)
    y = (jnp.einsum('btgp,gcp->btgc', s_re, c_re.astype(f32))
         - jnp.einsum('btgp,gcp->btgc', s_im, c_im.astype(f32))
         + d_skip.astype(f32).reshape(S5_GROUPS, S5_GROUP_CH) * uf).reshape(bsz, t, D_S5)
    g = jax.nn.gelu(y)
    out = g * jax.nn.sigmoid(g @ glu_w.astype(f32) + glu_b.astype(f32))
    return out.astype(u.dtype)


def token_mixer(h, w_in, mlstm_conv_w, mlstm_conv_b, mlstm_gate_b, ret_gn_w, mlstm_gn_w,
                s5_A_re, s5_A_im, s5_log_step, s5_B_re, s5_B_im, s5_C_re, s5_C_im, s5_D,
                s5_glu_w, s5_glu_b, w_out):
    bsz, t, _ = h.shape
    sizes = (RET_HEADS * RET_QK_DIM, RET_HEADS * RET_QK_DIM, D_RET, D_RET,
             2 * D_MLSTM, D_MLSTM, D_MLSTM, 2 * MLSTM_HEADS, D_S5)
    splits = np.cumsum(sizes)[:-1].tolist()
    proj = h @ w_in
    r_q, r_k, r_v, r_g, m_qk, m_v, m_o, m_gates, s_u = jnp.split(proj, splits, axis=-1)
    rq = rotary(r_q.reshape(bsz, t, RET_HEADS, RET_QK_DIM))
    rk = rotary(r_k.reshape(bsz, t, RET_HEADS, RET_QK_DIM))
    ret = retention(rq, rk, r_v.reshape(bsz, t, RET_HEADS, RET_V_DIM))
    ret_out = jax.nn.silu(r_g) * head_groupnorm(ret, ret_gn_w)
    m_qk = jax.nn.silu(causal_dwconv(m_qk, mlstm_conv_w, mlstm_conv_b))
    mq, mk = jnp.split(m_qk, 2, axis=-1)
    i_pre, f_pre = jnp.split(m_gates + mlstm_gate_b, 2, axis=-1)
    hm = mlstm(mq.reshape(bsz, t, MLSTM_HEADS, MLSTM_DIM), mk.reshape(bsz, t, MLSTM_HEADS, MLSTM_DIM),
               m_v.reshape(bsz, t, MLSTM_HEADS, MLSTM_DIM), i_pre, f_pre)
    mlstm_out = jax.nn.sigmoid(m_o) * head_groupnorm(hm, mlstm_gn_w)
    s5_out = s5(s_u, s5_A_re, s5_A_im, s5_log_step, s5_B_re, s5_B_im, s5_C_re, s5_C_im, s5_D,
                s5_glu_w, s5_glu_b)
    return jnp.concatenate([ret_out, mlstm_out, s5_out], axis=-1) @ w_out


def conv_ffn(h, w_up, conv_w, conv_b, w_down):
    up = causal_dwconv(h @ w_up, conv_w, conv_b)
    val, gate = jnp.split(up, 2, axis=-1)
    return (jax.nn.silu(gate) * val) @ w_down


def setup_inputs(seed: int = 0) -> dict:
    key = jax.random.key(seed)
    ks = jax.random.split(key, 32)
    f32 = jnp.float32
    nrm = lambda k, shape, scale: jax.random.normal(k, shape, f32) * scale
    x = nrm(ks[0], (BATCH, SEQ, D_MODEL), 1.0)
    norm1_w = 1.0 + nrm(ks[1], (DEPTH, D_MODEL), 0.02)
    w_in = nrm(ks[2], (DEPTH, D_MODEL, D_IN), D_MODEL ** -0.5)
    mlstm_conv_w = nrm(ks[3], (DEPTH, MLSTM_CONV, 2 * D_MLSTM), MLSTM_CONV ** -0.5)
    mlstm_conv_b = nrm(ks[4], (DEPTH, 2 * D_MLSTM), 0.01)
    i_bias = nrm(ks[5], (DEPTH, MLSTM_HEADS), 0.1)
    f_bias = jnp.linspace(3.0, 6.0, MLSTM_HEADS, dtype=f32)[None, :] + nrm(ks[6], (DEPTH, MLSTM_HEADS), 0.1)
    mlstm_gate_b = jnp.concatenate([i_bias, f_bias], axis=-1)
    ret_gn_w = 1.0 + nrm(ks[7], (DEPTH, D_RET), 0.02)
    mlstm_gn_w = 1.0 + nrm(ks[8], (DEPTH, D_MLSTM), 0.02)
    n_idx = jnp.arange(S5_STATE, dtype=f32)
    s5_A_re = -0.5 + nrm(ks[9], (DEPTH, S5_GROUPS, S5_STATE), 0.01)
    s5_A_im = jnp.pi * n_idx + nrm(ks[10], (DEPTH, S5_GROUPS, S5_STATE), 0.01)
    s5_log_step = jax.random.uniform(ks[11], (DEPTH, S5_GROUPS), f32, math.log(0.001), math.log(0.1))
    s5_B_re = nrm(ks[12], (DEPTH, S5_GROUPS, S5_STATE, S5_GROUP_CH), (2 * S5_GROUP_CH) ** -0.5)
    s5_B_im = nrm(ks[13], (DEPTH, S5_GROUPS, S5_STATE, S5_GROUP_CH), (2 * S5_GROUP_CH) ** -0.5)
    s5_C_re = nrm(ks[14], (DEPTH, S5_GROUPS, S5_GROUP_CH, S5_STATE), S5_STATE ** -0.5)
    s5_C_im = nrm(ks[15], (DEPTH, S5_GROUPS, S5_GROUP_CH, S5_STATE), S5_STATE ** -0.5)
    s5_D = nrm(ks[16], (DEPTH, D_S5), 1.0)
    s5_glu_w = nrm(ks[17], (DEPTH, D_S5, D_S5), D_S5 ** -0.5)
    s5_glu_b = nrm(ks[18], (DEPTH, D_S5), 0.01)
    w_out = nrm(ks[19], (DEPTH, D_MIX, D_MODEL), D_MIX ** -0.5)
    norm2_w = 1.0 + nrm(ks[20], (DEPTH, D_MODEL), 0.02)
    ffn_w_up = nrm(ks[21], (DEPTH, D_MODEL, 2 * D_FF), D_MODEL ** -0.5)
    ffn_conv_w = nrm(ks[22], (DEPTH, FFN_CONV, 2 * D_FF), FFN_CONV ** -0.5)
    ffn_conv_b = nrm(ks[23], (DEPTH, 2 * D_FF), 0.01)
    ffn_w_down = nrm(ks[24], (DEPTH, D_FF, D_MODEL), D_FF ** -0.5)
    final_norm_w = 1.0 + nrm(ks[25], (D_MODEL,), 0.02)
    return {"x": x, "norm1_w": norm1_w, "w_in": w_in, "mlstm_conv_w": mlstm_conv_w,
            "mlstm_conv_b": mlstm_conv_b, "mlstm_gate_b": mlstm_gate_b, "ret_gn_w": ret_gn_w,
            "mlstm_gn_w": mlstm_gn_w, "s5_A_re": s5_A_re, "s5_A_im": s5_A_im,
            "s5_log_step": s5_log_step, "s5_B_re": s5_B_re, "s5_B_im": s5_B_im,
            "s5_C_re": s5_C_re, "s5_C_im": s5_C_im, "s5_D": s5_D, "s5_glu_w": s5_glu_w,
            "s5_glu_b": s5_glu_b, "w_out": w_out, "norm2_w": norm2_w, "ffn_w_up": ffn_w_up,
            "ffn_conv_w": ffn_conv_w, "ffn_conv_b": ffn_conv_b, "ffn_w_down": ffn_w_down,
            "final_norm_w": final_norm_w}


def reference(x, norm1_w, w_in, mlstm_conv_w, mlstm_conv_b, mlstm_gate_b, ret_gn_w, mlstm_gn_w,
              s5_A_re, s5_A_im, s5_log_step, s5_B_re, s5_B_im, s5_C_re, s5_C_im, s5_D,
              s5_glu_w, s5_glu_b, w_out, norm2_w, ffn_w_up, ffn_conv_w, ffn_conv_b, ffn_w_down,
              final_norm_w):
    for l in range(DEPTH):
        h = rmsnorm(x, norm1_w[l])
        x = x + token_mixer(h, w_in[l], mlstm_conv_w[l], mlstm_conv_b[l], mlstm_gate_b[l],
                            ret_gn_w[l], mlstm_gn_w[l], s5_A_re[l], s5_A_im[l], s5_log_step[l],
                            s5_B_re[l], s5_B_im[l], s5_C_re[l], s5_C_im[l], s5_D[l],
                            s5_glu_w[l], s5_glu_b[l], w_out[l])
        h = rmsnorm(x, norm2_w[l])
        x = x + conv_ffn(h, ffn_w_up[l], ffn_conv_w[l], ffn_conv_b[l], ffn_w_down[l])
    return rmsnorm(x, final_norm_w)
```

```python
import functools

import jax
import jax.numpy as jnp
from jax import lax
from jax.experimental import pallas as pl
from jax.experimental.pallas import tpu as pltpu

RET_V_DIM = 128
RET_QK_DIM = 64
MLSTM_DIM = 128
S5_GROUP_CH = 16
S5_STATE = 64
ROPE_BASE = 10000.0
EPS = 1e-6

CHUNK = 128
S5_SUB = 16
SUBLANES = 8
LANES = 128

VMEM_LIMIT_BYTES = 56 * 1024 * 1024

F32 = jnp.float32
BF16 = jnp.bfloat16


def _params(n_grid):
    return pltpu.CompilerParams(
        dimension_semantics=("arbitrary",) * n_grid, vmem_limit_bytes=VMEM_LIMIT_BYTES)


def _rms(x, w):
    return x * lax.rsqrt(jnp.mean(x * x, axis=-1, keepdims=True) + EPS) * w


def _group_norm(o, w):
    mu = jnp.mean(o, axis=-1, keepdims=True)
    d = o - mu
    var = jnp.mean(d * d, axis=-1, keepdims=True)
    return d * lax.rsqrt(var + EPS) * w


def _dot(a, b):
    return jnp.dot(a, b, preferred_element_type=F32)


def _dot_nt(a, b):
    return lax.dot_general(a, b, (((1,), (1,)), ((), ())), preferred_element_type=F32)


def _dot_tn(a, b):
    return lax.dot_general(a, b, (((0,), (0,)), ((), ())), preferred_element_type=F32)


def _norm_matmul_kernel(x_ref, nw_ref, w_ref, o_ref, h_ref):
    @pl.when(pl.program_id(1) == 0)
    def _():
        h_ref[...] = _rms(x_ref[...], nw_ref[...]).astype(h_ref.dtype)

    o_ref[...] = _dot(h_ref[...], w_ref[...]).astype(o_ref.dtype)


def _norm_matmul(x, norm_w, w, *, tm, tn, out_dtype):
    n, d = x.shape
    n_out = w.shape[1]
    return pl.pallas_call(
        _norm_matmul_kernel,
        grid=(n // tm, n_out // tn),
        in_specs=[
            pl.BlockSpec((tm, d), lambda i, j: (i, 0)),
            pl.BlockSpec((1, d), lambda i, j: (0, 0)),
            pl.BlockSpec((d, tn), lambda i, j: (0, j)),
        ],
        out_specs=pl.BlockSpec((tm, tn), lambda i, j: (i, j)),
        out_shape=jax.ShapeDtypeStruct((n, n_out), out_dtype),
        scratch_shapes=[pltpu.VMEM((tm, d), BF16)],
        compiler_params=_params(2),
        name="norm_in_proj",
    )(x, norm_w.reshape(1, d), w)


def _retention_kernel(q_ref, k_ref, v_ref, g_ref, cos_ref, sin_ref, intra_ref, qd_ref, kd_ref,
                      cd_ref, gnw_ref, o_ref, st_ref, *, heads):
    @pl.when(pl.program_id(1) == 0)
    def _():
        st_ref[...] = jnp.zeros_like(st_ref)

    cos = cos_ref[...]
    sin = sin_ref[...]
    width = cos.shape[1]
    half = RET_QK_DIM // 2
    first_half = (lax.broadcasted_iota(jnp.int32, cos.shape, 1) % RET_QK_DIM) < half

    def rope(x):
        swapped = jnp.where(first_half, pltpu.roll(x, width - half, 1), pltpu.roll(x, half, 1))
        return x * cos + swapped * sin

    q = rope(q_ref[...]) * (RET_QK_DIM ** -0.5)
    k = rope(k_ref[...])
    for h in range(heads):
        qs = slice(h * RET_QK_DIM, (h + 1) * RET_QK_DIM)
        vs = slice(h * RET_V_DIM, (h + 1) * RET_V_DIM)
        qh = q[:, qs].astype(BF16)
        kh = k[:, qs]
        vh = v_ref[:, vs].astype(BF16)
        s = _dot_nt(qh, kh.astype(BF16)) * intra_ref[h]
        inner = _dot(s.astype(BF16), vh)
        st = st_ref[h]
        cross = _dot(qh, st.astype(BF16)) * qd_ref[h]
        st_ref[h] = st * cd_ref[h] + _dot_tn((kh * kd_ref[h]).astype(BF16), vh)
        y = _group_norm(inner + cross, gnw_ref[:, vs])
        g = g_ref[:, vs]
        o_ref[:, vs] = (g * jax.nn.sigmoid(g) * y).astype(o_ref.dtype)


def _retention_tables(t, heads):
    dh = RET_QK_DIM
    inv = ROPE_BASE ** (-jnp.arange(0, dh, 2, dtype=F32) / dh)
    ang = jnp.arange(t, dtype=F32)[:, None] * inv[None, :]
    cos, sin = jnp.cos(ang), jnp.sin(ang)
    cos_t = jnp.tile(jnp.concatenate([cos, cos], axis=-1), (1, heads))
    sin_t = jnp.tile(jnp.concatenate([-sin, sin], axis=-1), (1, heads))
    log_gamma = jnp.log1p(-jnp.exp2(-5.0 - jnp.arange(heads, dtype=F32)))
    idx = jnp.arange(CHUNK, dtype=F32)
    diff = idx[:, None] - idx[None, :]
    intra = jnp.where(diff[None] >= 0,
                      jnp.exp(jnp.maximum(diff, 0.0)[None] * log_gamma[:, None, None]), 0.0)
    q_decay = jnp.exp((idx[None, :] + 1.0) * log_gamma[:, None])
    k_decay = jnp.exp((CHUNK - 1.0 - idx)[None, :] * log_gamma[:, None])
    chunk_decay = jnp.exp(CHUNK * log_gamma)
    qd = jnp.broadcast_to(q_decay[:, :, None], (heads, CHUNK, RET_V_DIM))
    kd = jnp.broadcast_to(k_decay[:, :, None], (heads, CHUNK, RET_QK_DIM))
    cd = jnp.broadcast_to(chunk_decay[:, None, None], (heads, RET_QK_DIM, RET_V_DIM))
    return cos_t, sin_t, intra, qd, kd, cd


def _retention(proj, cols, gn_w, *, batch, t):
    n = proj.shape[0]
    d_ret = gn_w.shape[0]
    heads = d_ret // RET_V_DIM
    d_qk = heads * RET_QK_DIM
    nc = t // CHUNK
    cos_t, sin_t, intra, qd, kd, cd = _retention_tables(t, heads)
    row = lambda b, c: b * nc + c
    const3 = lambda b, c: (0, 0, 0)
    return pl.pallas_call(
        functools.partial(_retention_kernel, heads=heads),
        grid=(batch, nc),
        in_specs=[
            pl.BlockSpec((CHUNK, d_qk), lambda b, c: (row(b, c), cols["r_q"] // d_qk)),
            pl.BlockSpec((CHUNK, d_qk), lambda b, c: (row(b, c), cols["r_k"] // d_qk)),
            pl.BlockSpec((CHUNK, d_ret), lambda b, c: (row(b, c), cols["r_v"] // d_ret)),
            pl.BlockSpec((CHUNK, d_ret), lambda b, c: (row(b, c), cols["r_g"] // d_ret)),
            pl.BlockSpec((CHUNK, d_qk), lambda b, c: (c, 0)),
            pl.BlockSpec((CHUNK, d_qk), lambda b, c: (c, 0)),
            pl.BlockSpec(intra.shape, const3),
            pl.BlockSpec(qd.shape, const3),
            pl.BlockSpec(kd.shape, const3),
            pl.BlockSpec(cd.shape, const3),
            pl.BlockSpec((1, d_ret), lambda b, c: (0, 0)),
        ],
        out_specs=pl.BlockSpec((CHUNK, d_ret), lambda b, c: (row(b, c), 0)),
        out_shape=jax.ShapeDtypeStruct((n, d_ret), BF16),
        scratch_shapes=[pltpu.VMEM((heads, RET_QK_DIM, RET_V_DIM), F32)],
        compiler_params=_params(2),
        name="retention",
    )(proj, proj, proj, proj, cos_t, sin_t, intra, qd, kd, cd, gn_w.reshape(1, d_ret))


def _mlstm_kernel(qk_ref, v_ref, og_ref, gates_ref, cw_ref, cb_ref, gb_ref, gnw_ref, o_ref,
                  xbuf, c_st, n_st, m_st, *, heads):
    L = CHUNK
    d = MLSTM_DIM
    taps = cw_ref.shape[0]

    @pl.when(pl.program_id(1) == 0)
    def _():
        xbuf[0:SUBLANES, :] = jnp.zeros((SUBLANES, xbuf.shape[1]), F32)
        c_st[...] = jnp.zeros_like(c_st)
        n_st[...] = jnp.zeros_like(n_st)
        m_st[...] = jnp.zeros_like(m_st)

    x = qk_ref[...]
    xbuf[SUBLANES:SUBLANES + L, :] = x
    cw = cw_ref[...]
    y = cb_ref[...] + cw[taps - 1:taps, :] * x
    for s in range(1, taps):
        y = y + cw[taps - 1 - s:taps - s, :] * xbuf[SUBLANES - s:SUBLANES - s + L, :]
    xbuf[0:SUBLANES, :] = x[L - SUBLANES:L, :]
    a = y * jax.nn.sigmoid(y)
    d_m = heads * d
    q = a[:, :d_m]
    k = a[:, d_m:] * (d ** -0.5)

    gt = gates_ref[...] + gb_ref[...]
    lane = lax.broadcasted_iota(jnp.int32, gt.shape, 1)
    gate_tile = jnp.where(lane < heads, gt, jnp.where(lane < 2 * heads, jax.nn.log_sigmoid(gt), 0.0))
    r = lax.broadcasted_iota(jnp.int32, (L, L), 0)
    cidx = lax.broadcasted_iota(jnp.int32, (L, L), 1)
    causal = r >= cidx
    cum_col = jnp.dot(causal.astype(F32), gate_tile, precision=lax.Precision.HIGHEST,
                      preferred_element_type=F32)
    gate_rows = gate_tile.T
    cum_row = jnp.dot(gate_rows, (r <= cidx).astype(F32), precision=lax.Precision.HIGHEST,
                      preferred_element_type=F32)

    for h in range(heads):
        hs = slice(h * d, (h + 1) * d)
        f = heads + h
        cum_c = cum_col[:, f:f + 1]
        cum_r = cum_row[f:f + 1, :]
        ii_r = gate_rows[h:h + 1, :]
        ii_c = gate_tile[:, h:h + 1]
        m_prev = m_st[h:h + 1, 0:1]
        logw = jnp.where(causal, cum_c - cum_r + ii_r, -jnp.inf)
        inter = cum_c + m_prev
        m_t = jnp.maximum(inter, jnp.max(logw, axis=-1, keepdims=True))
        w = jnp.exp(logw - m_t)
        sc = jnp.exp(inter - m_t)
        qh = q[:, hs]
        kh = k[:, hs]
        qb = qh.astype(BF16)
        vb = v_ref[:, hs].astype(BF16)
        qk = _dot_nt(qb, kh.astype(BF16)) * w
        c_prev = c_st[h]
        n_prev = n_st[h:h + 1, :]
        num = _dot(qk.astype(BF16), vb) + sc * _dot(qb, c_prev.astype(BF16))
        den = jnp.sum(qk, axis=-1, keepdims=True) + sc * jnp.sum(qh * n_prev, axis=-1, keepdims=True)
        hv = num / jnp.maximum(jnp.abs(den), jnp.exp(-m_t))
        last = cum_c[L - 1:L, :]
        logw_end = last - cum_c + ii_c
        m_new = jnp.maximum(last + m_prev, jnp.max(logw_end, axis=0, keepdims=True))
        kw = kh * jnp.exp(logw_end - m_new)
        decay = jnp.exp(last + m_prev - m_new)
        c_st[h] = decay * c_prev + _dot_tn(kw.astype(BF16), vb)
        n_st[h:h + 1, :] = decay * n_prev + jnp.sum(kw, axis=0, keepdims=True)
        m_st[h:h + 1, :] = jnp.broadcast_to(m_new, (1, m_st.shape[1]))
        yn = _group_norm(hv, gnw_ref[:, hs])
        o_ref[:, hs] = (jax.nn.sigmoid(og_ref[:, hs]) * yn).astype(o_ref.dtype)


def _mlstm(proj, cols, conv_w, conv_b, gate_b, gn_w, *, batch, t):
    n = proj.shape[0]
    d_m = gn_w.shape[0]
    heads = d_m // MLSTM_DIM
    nc = t // CHUNK
    taps = conv_w.shape[0]
    gb = jnp.zeros((1, LANES), F32).at[0, :2 * heads].set(gate_b)
    row = lambda b, c: b * nc + c
    const2 = lambda b, c: (0, 0)
    return pl.pallas_call(
        functools.partial(_mlstm_kernel, heads=heads),
        grid=(batch, nc),
        in_specs=[
            pl.BlockSpec((CHUNK, 2 * d_m), lambda b, c: (row(b, c), cols["m_qk"] // (2 * d_m))),
            pl.BlockSpec((CHUNK, d_m), lambda b, c: (row(b, c), cols["m_v"] // d_m)),
            pl.BlockSpec((CHUNK, d_m), lambda b, c: (row(b, c), cols["m_o"] // d_m)),
            pl.BlockSpec((CHUNK, LANES), lambda b, c: (row(b, c), cols["m_gates"] // LANES)),
            pl.BlockSpec((taps, 2 * d_m), const2),
            pl.BlockSpec((1, 2 * d_m), const2),
            pl.BlockSpec((1, LANES), const2),
            pl.BlockSpec((1, d_m), const2),
        ],
        out_specs=pl.BlockSpec((CHUNK, d_m), lambda b, c: (row(b, c), 0)),
        out_shape=jax.ShapeDtypeStruct((n, d_m), BF16),
        scratch_shapes=[
            pltpu.VMEM((SUBLANES + CHUNK, 2 * d_m), F32),
            pltpu.VMEM((heads, MLSTM_DIM, MLSTM_DIM), F32),
            pltpu.VMEM((SUBLANES, MLSTM_DIM), F32),
            pltpu.VMEM((SUBLANES, LANES), F32),
        ],
        compiler_params=_params(2),
        name="mlstm",
    )(proj, proj, proj, proj, conv_w, conv_b.reshape(1, -1), gb, gn_w.reshape(1, d_m))


def _s5_kernel(u_ref, t_ref, wre_ref, wim_ref, v_ref, are_ref, aim_ref, y_ref,
               ere, eim, sre, sim, *, batch):
    u = u_ref[...]
    ere[...] = _dot(u, wre_ref[...])
    eim[...] = _dot(u, wim_ref[...])
    steps = ere.shape[0] // batch
    ar = jnp.broadcast_to(are_ref[...], (batch, LANES))
    ai = jnp.broadcast_to(aim_ref[...], (batch, LANES))

    def body(c, carry):
        re, im = carry
        rows = pl.ds(pl.multiple_of(c * batch, batch), batch)
        sre[rows, :] = re
        sim[rows, :] = im
        return ar * re - ai * im + ere[rows, :], ar * im + ai * re + eim[rows, :]

    zero = jnp.zeros((batch, LANES), F32)
    lax.fori_loop(0, steps, body, (zero, zero))
    s = jnp.concatenate([sre[...], sim[...]], axis=1).astype(BF16)
    w = u.shape[1] // 2
    for gi in range(2):
        cs = slice(gi * w, (gi + 1) * w)
        y_ref[:, cs] = _dot(u[:, cs], t_ref[gi]) + _dot(s, v_ref[gi])


def _s5_weights(a_re, a_im, log_step, b_re, b_im, c_re, c_im):
    hp = lax.Precision.HIGHEST
    g, p = a_re.shape
    ch = b_re.shape[-1]
    s = S5_SUB
    ar, ai = a_re.astype(F32), a_im.astype(F32)
    step = jnp.exp(log_step.astype(F32))[:, None]
    mag = jnp.exp(ar * step)
    abar_re, abar_im = mag * jnp.cos(ai * step), mag * jnp.sin(ai * step)
    xr, xi = abar_re - 1.0, abar_im
    den = ar * ar + ai * ai
    fr, fi = (xr * ar + xi * ai) / den, (xi * ar - xr * ai) / den
    br, bi = b_re.astype(F32), b_im.astype(F32)
    bbar_re = fr[..., None] * br - fi[..., None] * bi
    bbar_im = fr[..., None] * bi + fi[..., None] * br
    j = jnp.arange(s + 1, dtype=F32)[:, None, None]
    pmag = jnp.exp(ar[None] * step[None] * j)
    pr, pi = pmag * jnp.cos(ai[None] * step[None] * j), pmag * jnp.sin(ai[None] * step[None] * j)
    cr, ci = c_re.astype(F32), c_im.astype(F32)
    cp_re = cr[None] * pr[:, :, None, :] - ci[None] * pi[:, :, None, :]
    cp_im = cr[None] * pi[:, :, None, :] + ci[None] * pr[:, :, None, :]
    kern = (jnp.einsum('jgcp,gpd->jgcd', cp_re, bbar_re, precision=hp)
            - jnp.einsum('jgcp,gpd->jgcd', cp_im, bbar_im, precision=hp))
    tt = jnp.arange(s)
    lag = tt[None, :] - tt[:, None]
    toe = jnp.where((lag >= 0)[:, :, None, None, None], kern[jnp.clip(lag, 0, s)], 0.0)
    t_mat = toe.transpose(2, 0, 4, 1, 3).reshape(g, s * ch, s * ch)
    prr, pir = pr[s - 1 - tt], pi[s - 1 - tt]
    w_re = prr[..., None] * bbar_re[None] - pir[..., None] * bbar_im[None]
    w_im = prr[..., None] * bbar_im[None] + pir[..., None] * bbar_re[None]
    w_re = w_re.transpose(1, 0, 3, 2).reshape(g, s * ch, p)
    w_im = w_im.transpose(1, 0, 3, 2).reshape(g, s * ch, p)
    v_re = cp_re[1:].transpose(1, 3, 0, 2).reshape(g, p, s * ch)
    v_im = (-cp_im[1:]).transpose(1, 3, 0, 2).reshape(g, p, s * ch)
    return t_mat, w_re, w_im, v_re, v_im, pr[s], pi[s]


def _s5(proj, cols, a_re, a_im, log_step, b_re, b_im, c_re, c_im, *, batch, t):
    g, p = a_re.shape
    ch = b_re.shape[-1]
    s = S5_SUB
    pairs = g // 2
    w = s * ch
    assert 2 * p == LANES
    t_mat, w_re, w_im, v_re, v_im, a16_re, a16_im = _s5_weights(a_re, a_im, log_step, b_re, b_im, c_re, c_im)
    zp = jnp.zeros((pairs, w, p), F32)
    wre_p = jnp.concatenate([jnp.concatenate([w_re[0::2], zp], axis=2),
                             jnp.concatenate([zp, w_re[1::2]], axis=2)], axis=1)
    wim_p = jnp.concatenate([jnp.concatenate([w_im[0::2], zp], axis=2),
                             jnp.concatenate([zp, w_im[1::2]], axis=2)], axis=1)
    zv = jnp.zeros((pairs, p, w), F32)
    v_even = jnp.concatenate([v_re[0::2], zv, v_im[0::2], zv], axis=1)
    v_odd = jnp.concatenate([zv, v_re[1::2], zv, v_im[1::2]], axis=1)
    v_p = jnp.stack([v_even, v_odd], axis=1).reshape(g, 4 * p, w)
    are_p = jnp.concatenate([a16_re[0::2], a16_re[1::2]], axis=1).reshape(pairs, 1, 2 * p)
    aim_p = jnp.concatenate([a16_im[0::2], a16_im[1::2]], axis=1).reshape(pairs, 1, 2 * p)

    d_s5 = g * ch
    c0 = cols["s_u"]
    nsub = t // s
    m = nsub * batch
    u = proj[:, c0:c0 + d_s5].reshape(batch, nsub, s, pairs, 2, ch)
    u = u.transpose(3, 1, 0, 4, 2, 5).reshape(pairs, m, 2 * w).astype(BF16)
    y = pl.pallas_call(
        functools.partial(_s5_kernel, batch=batch),
        grid=(pairs,),
        in_specs=[
            pl.BlockSpec((None, m, 2 * w), lambda r: (r, 0, 0)),
            pl.BlockSpec((2, w, w), lambda r: (r, 0, 0)),
            pl.BlockSpec((None, 2 * w, 2 * p), lambda r: (r, 0, 0)),
            pl.BlockSpec((None, 2 * w, 2 * p), lambda r: (r, 0, 0)),
            pl.BlockSpec((2, 4 * p, w), lambda r: (r, 0, 0)),
            pl.BlockSpec((None, 1, 2 * p), lambda r: (r, 0, 0)),
            pl.BlockSpec((None, 1, 2 * p), lambda r: (r, 0, 0)),
        ],
        out_specs=pl.BlockSpec((None, m, 2 * w), lambda r: (r, 0, 0)),
        out_shape=jax.ShapeDtypeStruct((pairs, m, 2 * w), F32),
        scratch_shapes=[pltpu.VMEM((m, 2 * p), F32)] * 4,
        compiler_params=_params(1),
        name="s5_scan",
    )(u, t_mat.astype(BF16), wre_p.astype(BF16), wim_p.astype(BF16), v_p.astype(BF16), are_p, aim_p)
    y = y.reshape(pairs, nsub, batch, 2, s, ch).transpose(2, 1, 4, 0, 3, 5)
    return y.reshape(batch * t, d_s5)


def _s5_glu_kernel(y_ref, u_ref, d_ref, w_ref, b_ref, o_ref):
    y = y_ref[...] + d_ref[...] * u_ref[...]
    g = jax.nn.gelu(y)
    z = _dot(g.astype(BF16), w_ref[...]) + b_ref[...]
    o_ref[...] = (g * jax.nn.sigmoid(z)).astype(o_ref.dtype)


def _s5_glu(y, proj, cols, d_skip, glu_w, glu_b, *, tm):
    n, d_s5 = y.shape
    const2 = lambda i: (0, 0)
    return pl.pallas_call(
        _s5_glu_kernel,
        grid=(n // tm,),
        in_specs=[
            pl.BlockSpec((tm, d_s5), lambda i: (i, 0)),
            pl.BlockSpec((tm, d_s5), lambda i: (i, cols["s_u"] // d_s5)),
            pl.BlockSpec((1, d_s5), const2),
            pl.BlockSpec((d_s5, d_s5), const2),
            pl.BlockSpec((1, d_s5), const2),
        ],
        out_specs=pl.BlockSpec((tm, d_s5), lambda i: (i, 0)),
        out_shape=jax.ShapeDtypeStruct((n, d_s5), BF16),
        compiler_params=_params(1),
        name="s5_glu",
    )(y, proj, d_skip.reshape(1, d_s5), glu_w.astype(BF16), glu_b.reshape(1, d_s5))


def _out_proj_kernel(x_ref, a_ref, b_ref, c_ref, wa_ref, wb_ref, wc_ref, o_ref):
    acc = _dot(a_ref[...], wa_ref[...])
    acc = acc + _dot(b_ref[...], wb_ref[...])
    acc = acc + _dot(c_ref[...], wc_ref[...])
    o_ref[...] = x_ref[...] + acc


def _out_proj(x, ret, ml, s5o, w_out, *, tm):
    n, d = x.shape
    da, db, dc = ret.shape[1], ml.shape[1], s5o.shape[1]
    assert da == db and (da + db) % dc == 0
    return pl.pallas_call(
        _out_proj_kernel,
        grid=(n // tm,),
        in_specs=[
            pl.BlockSpec((tm, d), lambda i: (i, 0)),
            pl.BlockSpec((tm, da), lambda i: (i, 0)),
            pl.BlockSpec((tm, db), lambda i: (i, 0)),
            pl.BlockSpec((tm, dc), lambda i: (i, 0)),
            pl.BlockSpec((da, d), lambda i: (0, 0)),
            pl.BlockSpec((db, d), lambda i: (1, 0)),
            pl.BlockSpec((dc, d), lambda i: ((da + db) // dc, 0)),
        ],
        out_specs=pl.BlockSpec((tm, d), lambda i: (i, 0)),
        out_shape=jax.ShapeDtypeStruct((n, d), F32),
        compiler_params=_params(1),
        name="out_proj",
    )(x, ret, ml, s5o, w_out, w_out, w_out)


def _ffn_kernel(x_ref, nw_ref, wv_ref, wg_ref, cwv_ref, cwg_ref, cbv_ref, cbg_ref, wd_ref, fnw_ref,
                o_ref, h_ref, acc_ref, ubv, ubg, halo_v, halo_g, *, final_norm):
    ti = pl.program_id(1)
    j = pl.program_id(2)
    tm = x_ref.shape[0]

    @pl.when(j == 0)
    def _():
        h_ref[...] = _rms(x_ref[...], nw_ref[...]).astype(h_ref.dtype)
        acc_ref[...] = jnp.zeros_like(acc_ref)

    h = h_ref[...]

    def conv_branch(w_ref, cw_ref, cb_ref, ub, halo):
        up = _dot(h, w_ref[...])
        taps = cw_ref.shape[0]
        @pl.when(ti == 0)
        def _():
            ub[0:SUBLANES, :] = jnp.zeros((SUBLANES, ub.shape[1]), F32)

        @pl.when(ti > 0)
        def _():
            ub[0:SUBLANES, :] = halo[j]

        ub[SUBLANES:SUBLANES + tm, :] = up
        halo[j] = up[tm - SUBLANES:tm, :]
        cw = cw_ref[...]
        y = cb_ref[...] + cw[taps - 1:taps, :] * up
        for s in range(1, taps):
            y = y + cw[taps - 1 - s:taps - s, :] * ub[SUBLANES - s:SUBLANES - s + tm, :]
        return y

    val = conv_branch(wv_ref, cwv_ref, cbv_ref, ubv, halo_v)
    gate = conv_branch(wg_ref, cwg_ref, cbg_ref, ubg, halo_g)
    act = (gate * jax.nn.sigmoid(gate) * val).astype(BF16)
    acc_ref[...] += _dot(act, wd_ref[...])

    @pl.when(j == pl.num_programs(2) - 1)
    def _():
        y = x_ref[...] + acc_ref[...]
        if final_norm:
            y = _rms(y, fnw_ref[...])
        o_ref[...] = y


def _ffn(x, norm_w, w_up, conv_w, conv_b, w_down, final_w, *, batch, t, tm, tf, final_norm):
    n, d = x.shape
    d_ff = w_down.shape[0]
    nj = d_ff // tf
    nt = t // tm
    taps = conv_w.shape[0]
    row = lambda b, i, j: (b * nt + i, 0)
    const2 = lambda b, i, j: (0, 0)
    cb = conv_b.reshape(1, -1)
    return pl.pallas_call(
        functools.partial(_ffn_kernel, final_norm=final_norm),
        grid=(batch, nt, nj),
        in_specs=[
            pl.BlockSpec((tm, d), row),
            pl.BlockSpec((1, d), const2),
            pl.BlockSpec((d, tf), lambda b, i, j: (0, j)),
            pl.BlockSpec((d, tf), lambda b, i, j: (0, j + nj)),
            pl.BlockSpec((taps, tf), lambda b, i, j: (0, j)),
            pl.BlockSpec((taps, tf), lambda b, i, j: (0, j + nj)),
            pl.BlockSpec((1, tf), lambda b, i, j: (0, j)),
            pl.BlockSpec((1, tf), lambda b, i, j: (0, j + nj)),
            pl.BlockSpec((tf, d), lambda b, i, j: (j, 0)),
            pl.BlockSpec((1, d), const2),
        ],
        out_specs=pl.BlockSpec((tm, d), row),
        out_shape=jax.ShapeDtypeStruct((n, d), F32),
        scratch_shapes=[
            pltpu.VMEM((tm, d), BF16),
            pltpu.VMEM((tm, d), F32),
            pltpu.VMEM((SUBLANES + tm, tf), F32),
            pltpu.VMEM((SUBLANES + tm, tf), F32),
            pltpu.VMEM((nj, SUBLANES, tf), F32),
            pltpu.VMEM((nj, SUBLANES, tf), F32),
        ],
        compiler_params=_params(3),
        name="conv_ffn",
    )(x, norm_w.reshape(1, d), w_up, w_up, conv_w, conv_w, cb, cb, w_down, final_w.reshape(1, d))


def _proj_layout(d_ret, d_m, d_s5, heads_r, heads_m):
    d_qk = heads_r * RET_QK_DIM
    src_sizes = [("r_q", d_qk), ("r_k", d_qk), ("r_v", d_ret), ("r_g", d_ret), ("m_qk", 2 * d_m),
                 ("m_v", d_m), ("m_o", d_m), ("m_gates", 2 * heads_m), ("s_u", d_s5)]
    src, off = {}, 0
    for name, size in src_sizes:
        src[name] = (off, size)
        off += size
    order = ["m_qk", "r_v", "r_g", "m_v", "m_o", "r_q", "r_k", "m_gates", "s_u"]
    cols, pieces, off = {}, [], 0
    for name in order:
        start, size = src[name]
        align = {"m_gates": LANES, "s_u": d_s5}.get(name, size)
        pad = (-off) % align
        if pad:
            pieces.append((None, pad))
            off += pad
        cols[name] = off
        pieces.append((start, size))
        off += size
        if name == "m_gates":
            pieces.append((None, LANES - size))
            off += LANES - size
    return cols, pieces, off


def _pick(n, pref):
    return pref if n % pref == 0 else n


def kernel(x, norm1_w, w_in, mlstm_conv_w, mlstm_conv_b, mlstm_gate_b, ret_gn_w, mlstm_gn_w,
           s5_A_re, s5_A_im, s5_log_step, s5_B_re, s5_B_im, s5_C_re, s5_C_im, s5_D,
           s5_glu_w, s5_glu_b, w_out, norm2_w, ffn_w_up, ffn_conv_w, ffn_conv_b, ffn_w_down,
           final_norm_w):
    batch, t, d = x.shape
    depth = w_in.shape[0]
    n = batch * t
    d_ret, d_m, d_s5 = ret_gn_w.shape[1], mlstm_gn_w.shape[1], s5_D.shape[1]
    heads_r, heads_m = d_ret // RET_V_DIM, d_m // MLSTM_DIM
    cols, pieces, width = _proj_layout(d_ret, d_m, d_s5, heads_r, heads_m)
    tn = 1536
    width_p = -(-width // tn) * tn
    tm = _pick(t, 512)
    tf = _pick(ffn_w_down.shape[1], 512)

    xf = x.reshape(n, d)
    for l in range(depth):
        w_l = w_in[l]
        parts = [jnp.zeros((d, size), BF16) if start is None else w_l[:, start:start + size].astype(BF16)
                 for start, size in pieces]
        if width_p > width:
            parts.append(jnp.zeros((d, width_p - width), BF16))
        w_perm = jnp.concatenate(parts, axis=1)
        proj = _norm_matmul(xf, norm1_w[l], w_perm, tm=tm, tn=tn, out_dtype=F32)
        ret = _retention(proj, cols, ret_gn_w[l], batch=batch, t=t)
        ml = _mlstm(proj, cols, mlstm_conv_w[l], mlstm_conv_b[l], mlstm_gate_b[l], mlstm_gn_w[l],
                    batch=batch, t=t)
        y5 = _s5(proj, cols, s5_A_re[l], s5_A_im[l], s5_log_step[l], s5_B_re[l], s5_B_im[l],
                 s5_C_re[l], s5_C_im[l], batch=batch, t=t)
        s5o = _s5_glu(y5, proj, cols, s5_D[l], s5_glu_w[l], s5_glu_b[l], tm=tm)
        xf = _out_proj(xf, ret, ml, s5o, w_out[l].astype(BF16), tm=_pick(t, 256))
        xf = _ffn(xf, norm2_w[l], ffn_w_up[l].astype(BF16), ffn_conv_w[l], ffn_conv_b[l],
                  ffn_w_down[l].astype(BF16), final_norm_w, batch=batch, t=t, tm=tm, tf=tf,
                  final_norm=(l == depth - 1))
    return xf.reshape(batch, t, d)
```

```python
import functools

import jax
import jax.numpy as jnp
from jax import lax
from jax.experimental import pallas as pl
from jax.experimental.pallas import tpu as pltpu

RET_V_DIM = 128
RET_QK_DIM = 64
MLSTM_DIM = 128
S5_GROUP_CH = 16
S5_STATE = 64
ROPE_BASE = 10000.0
EPS = 1e-6

CHUNK = 128
S5_SUB = 16
SUBLANES = 8
LANES = 128

VMEM_LIMIT_BYTES = 56 * 1024 * 1024

F32 = jnp.float32
BF16 = jnp.bfloat16


def _params(n_grid):
    return pltpu.CompilerParams(
        dimension_semantics=("arbitrary",) * n_grid, vmem_limit_bytes=VMEM_LIMIT_BYTES)


def _rms(x, w):
    return x * lax.rsqrt(jnp.mean(x * x, axis=-1, keepdims=True) + EPS) * w


def _group_norm(o, w):
    mu = jnp.mean(o, axis=-1, keepdims=True)
    d = o - mu
    var = jnp.mean(d * d, axis=-1, keepdims=True)
    return d * lax.rsqrt(var + EPS) * w


def _dot(a, b):
    return jnp.dot(a, b, preferred_element_type=F32)


def _dot_nt(a, b):
    return lax.dot_general(a, b, (((1,), (1,)), ((), ())), preferred_element_type=F32)


def _dot_tn(a, b):
    return lax.dot_general(a, b, (((0,), (0,)), ((), ())), preferred_element_type=F32)


def _norm_matmul_kernel(x_ref, nw_ref, w_ref, o_ref, u_ref, h_ref):
    j = pl.program_id(1)

    @pl.when(j == 0)
    def _():
        h_ref[...] = _rms(x_ref[...], nw_ref[...]).astype(h_ref.dtype)

    res = _dot(h_ref[...], w_ref[...])
    o_ref[...] = res.astype(o_ref.dtype)

    @pl.when(j == pl.num_programs(1) - 1)
    def _():
        slabs = u_ref.shape[0]
        base = res.shape[1] - slabs * LANES
        for q in range(slabs):
            u_ref[q] = res[:, base + q * LANES:base + (q + 1) * LANES].astype(u_ref.dtype)


def _norm_matmul(x, norm_w, w, *, tm, tn, d_s5):
    n, d = x.shape
    n_out = w.shape[1]
    slabs = d_s5 // LANES
    return pl.pallas_call(
        _norm_matmul_kernel,
        grid=(n // tm, n_out // tn),
        in_specs=[
            pl.BlockSpec((tm, d), lambda i, j: (i, 0)),
            pl.BlockSpec((1, d), lambda i, j: (0, 0)),
            pl.BlockSpec((d, tn), lambda i, j: (0, j)),
        ],
        out_specs=[
            pl.BlockSpec((tm, tn), lambda i, j: (i, j)),
            pl.BlockSpec((slabs, tm, LANES), lambda i, j: (0, i, 0)),
        ],
        out_shape=[
            jax.ShapeDtypeStruct((n, n_out), F32),
            jax.ShapeDtypeStruct((slabs, n, LANES), BF16),
        ],
        scratch_shapes=[pltpu.VMEM((tm, d), BF16)],
        compiler_params=_params(2),
        name="norm_in_proj",
    )(x, norm_w.reshape(1, d), w)


def _retention_kernel(q_ref, k_ref, v_ref, g_ref, cos_ref, sin_ref, intra_ref, qd_ref, kd_ref,
                      cd_ref, gnw_ref, o_ref, st_ref, *, heads):
    @pl.when(pl.program_id(1) == 0)
    def _():
        st_ref[...] = jnp.zeros_like(st_ref)

    cos = cos_ref[...]
    sin = sin_ref[...]
    width = cos.shape[1]
    half = RET_QK_DIM // 2
    first_half = (lax.broadcasted_iota(jnp.int32, cos.shape, 1) % RET_QK_DIM) < half

    def rope(x):
        swapped = jnp.where(first_half, pltpu.roll(x, width - half, 1), pltpu.roll(x, half, 1))
        return x * cos + swapped * sin

    q = rope(q_ref[...]) * (RET_QK_DIM ** -0.5)
    k = rope(k_ref[...])
    for h in range(heads):
        qs = slice(h * RET_QK_DIM, (h + 1) * RET_QK_DIM)
        vs = slice(h * RET_V_DIM, (h + 1) * RET_V_DIM)
        qh = q[:, qs].astype(BF16)
        kh = k[:, qs]
        vh = v_ref[:, vs].astype(BF16)
        s = _dot_nt(qh, kh.astype(BF16)) * intra_ref[h]
        inner = _dot(s.astype(BF16), vh)
        st = st_ref[h]
        cross = _dot(qh, st.astype(BF16)) * qd_ref[h]
        st_ref[h] = st * cd_ref[h] + _dot_tn((kh * kd_ref[h]).astype(BF16), vh)
        y = _group_norm(inner + cross, gnw_ref[:, vs])
        g = g_ref[:, vs]
        o_ref[:, vs] = (g * jax.nn.sigmoid(g) * y).astype(o_ref.dtype)


def _retention_tables(t, heads):
    dh = RET_QK_DIM
    inv = ROPE_BASE ** (-jnp.arange(0, dh, 2, dtype=F32) / dh)
    ang = jnp.arange(t, dtype=F32)[:, None] * inv[None, :]
    cos, sin = jnp.cos(ang), jnp.sin(ang)
    cos_t = jnp.tile(jnp.concatenate([cos, cos], axis=-1), (1, heads))
    sin_t = jnp.tile(jnp.concatenate([-sin, sin], axis=-1), (1, heads))
    log_gamma = jnp.log1p(-jnp.exp2(-5.0 - jnp.arange(heads, dtype=F32)))
    idx = jnp.arange(CHUNK, dtype=F32)
    diff = idx[:, None] - idx[None, :]
    intra = jnp.where(diff[None] >= 0,
                      jnp.exp(jnp.maximum(diff, 0.0)[None] * log_gamma[:, None, None]), 0.0)
    q_decay = jnp.exp((idx[None, :] + 1.0) * log_gamma[:, None])
    k_decay = jnp.exp((CHUNK - 1.0 - idx)[None, :] * log_gamma[:, None])
    chunk_decay = jnp.exp(CHUNK * log_gamma)
    qd = jnp.broadcast_to(q_decay[:, :, None], (heads, CHUNK, RET_V_DIM))
    kd = jnp.broadcast_to(k_decay[:, :, None], (heads, CHUNK, RET_QK_DIM))
    cd = jnp.broadcast_to(chunk_decay[:, None, None], (heads, RET_QK_DIM, RET_V_DIM))
    return cos_t, sin_t, intra, qd, kd, cd


def _retention(proj, cols, gn_w, *, batch, t):
    n = proj.shape[0]
    d_ret = gn_w.shape[0]
    heads = d_ret // RET_V_DIM
    d_qk = heads * RET_QK_DIM
    nc = t // CHUNK
    cos_t, sin_t, intra, qd, kd, cd = _retention_tables(t, heads)
    row = lambda b, c: b * nc + c
    const3 = lambda b, c: (0, 0, 0)
    return pl.pallas_call(
        functools.partial(_retention_kernel, heads=heads),
        grid=(batch, nc),
        in_specs=[
            pl.BlockSpec((CHUNK, d_qk), lambda b, c: (row(b, c), cols["r_q"] // d_qk)),
            pl.BlockSpec((CHUNK, d_qk), lambda b, c: (row(b, c), cols["r_k"] // d_qk)),
            pl.BlockSpec((CHUNK, d_ret), lambda b, c: (row(b, c), cols["r_v"] // d_ret)),
            pl.BlockSpec((CHUNK, d_ret), lambda b, c: (row(b, c), cols["r_g"] // d_ret)),
            pl.BlockSpec((CHUNK, d_qk), lambda b, c: (c, 0)),
            pl.BlockSpec((CHUNK, d_qk), lambda b, c: (c, 0)),
            pl.BlockSpec(intra.shape, const3),
            pl.BlockSpec(qd.shape, const3),
            pl.BlockSpec(kd.shape, const3),
            pl.BlockSpec(cd.shape, const3),
            pl.BlockSpec((1, d_ret), lambda b, c: (0, 0)),
        ],
        out_specs=pl.BlockSpec((CHUNK, d_ret), lambda b, c: (row(b, c), 0)),
        out_shape=jax.ShapeDtypeStruct((n, d_ret), BF16),
        scratch_shapes=[pltpu.VMEM((heads, RET_QK_DIM, RET_V_DIM), F32)],
        compiler_params=_params(2),
        name="retention",
    )(proj, proj, proj, proj, cos_t, sin_t, intra, qd, kd, cd, gn_w.reshape(1, d_ret))


def _mlstm_kernel(qk_ref, v_ref, og_ref, gates_ref, cw_ref, cb_ref, gb_ref, gnw_ref, o_ref,
                  xbuf, c_st, n_st, m_st, *, heads):
    L = CHUNK
    d = MLSTM_DIM
    taps = cw_ref.shape[0]

    @pl.when(pl.program_id(1) == 0)
    def _():
        xbuf[0:SUBLANES, :] = jnp.zeros((SUBLANES, xbuf.shape[1]), F32)
        c_st[...] = jnp.zeros_like(c_st)
        n_st[...] = jnp.zeros_like(n_st)
        m_st[...] = jnp.zeros_like(m_st)

    x = qk_ref[...]
    xbuf[SUBLANES:SUBLANES + L, :] = x
    cw = cw_ref[...]
    y = cb_ref[...] + cw[taps - 1:taps, :] * x
    for s in range(1, taps):
        y = y + cw[taps - 1 - s:taps - s, :] * xbuf[SUBLANES - s:SUBLANES - s + L, :]
    xbuf[0:SUBLANES, :] = x[L - SUBLANES:L, :]
    a = y * jax.nn.sigmoid(y)
    d_m = heads * d
    q = a[:, :d_m]
    k = a[:, d_m:] * (d ** -0.5)

    gt = gates_ref[...] + gb_ref[...]
    lane = lax.broadcasted_iota(jnp.int32, gt.shape, 1)
    gate_tile = jnp.where(lane < heads, gt, jnp.where(lane < 2 * heads, jax.nn.log_sigmoid(gt), 0.0))
    r = lax.broadcasted_iota(jnp.int32, (L, L), 0)
    cidx = lax.broadcasted_iota(jnp.int32, (L, L), 1)
    causal = r >= cidx
    cum_col = jnp.dot(causal.astype(F32), gate_tile, precision=lax.Precision.HIGHEST,
                      preferred_element_type=F32)
    gate_rows = gate_tile.T
    cum_row = jnp.dot(gate_rows, (r <= cidx).astype(F32), precision=lax.Precision.HIGHEST,
                      preferred_element_type=F32)

    for h in range(heads):
        hs = slice(h * d, (h + 1) * d)
        f = heads + h
        cum_c = cum_col[:, f:f + 1]
        cum_r = cum_row[f:f + 1, :]
        ii_r = gate_rows[h:h + 1, :]
        ii_c = gate_tile[:, h:h + 1]
        m_prev = m_st[h:h + 1, 0:1]
        logw = jnp.where(causal, cum_c - cum_r + ii_r, -jnp.inf)
        inter = cum_c + m_prev
        m_t = jnp.maximum(inter, jnp.max(logw, axis=-1, keepdims=True))
        w = jnp.exp(logw - m_t)
        sc = jnp.exp(inter - m_t)
        qh = q[:, hs]
        kh = k[:, hs]
        qb = qh.astype(BF16)
        vb = v_ref[:, hs].astype(BF16)
        qk = _dot_nt(qb, kh.astype(BF16)) * w
        c_prev = c_st[h]
        n_prev = n_st[h:h + 1, :]
        num = _dot(qk.astype(BF16), vb) + sc * _dot(qb, c_prev.astype(BF16))
        den = jnp.sum(qk, axis=-1, keepdims=True) + sc * jnp.sum(qh * n_prev, axis=-1, keepdims=True)
        hv = num / jnp.maximum(jnp.abs(den), jnp.exp(-m_t))
        last = cum_c[L - 1:L, :]
        logw_end = last - cum_c + ii_c
        m_new = jnp.maximum(last + m_prev, jnp.max(logw_end, axis=0, keepdims=True))
        kw = kh * jnp.exp(logw_end - m_new)
        decay = jnp.exp(last + m_prev - m_new)
        c_st[h] = decay * c_prev + _dot_tn(kw.astype(BF16), vb)
        n_st[h:h + 1, :] = decay * n_prev + jnp.sum(kw, axis=0, keepdims=True)
        m_st[h:h + 1, :] = jnp.broadcast_to(m_new, (1, m_st.shape[1]))
        yn = _group_norm(hv, gnw_ref[:, hs])
        o_ref[:, hs] = (jax.nn.sigmoid(og_ref[:, hs]) * yn).astype(o_ref.dtype)


def _mlstm(proj, cols, conv_w, conv_b, gate_b, gn_w, *, batch, t):
    n = proj.shape[0]
    d_m = gn_w.shape[0]
    heads = d_m // MLSTM_DIM
    nc = t // CHUNK
    taps = conv_w.shape[0]
    gb = jnp.zeros((1, LANES), F32).at[0, :2 * heads].set(gate_b)
    row = lambda b, c: b * nc + c
    const2 = lambda b, c: (0, 0)
    return pl.pallas_call(
        functools.partial(_mlstm_kernel, heads=heads),
        grid=(batch, nc),
        in_specs=[
            pl.BlockSpec((CHUNK, 2 * d_m), lambda b, c: (row(b, c), cols["m_qk"] // (2 * d_m))),
            pl.BlockSpec((CHUNK, d_m), lambda b, c: (row(b, c), cols["m_v"] // d_m)),
            pl.BlockSpec((CHUNK, d_m), lambda b, c: (row(b, c), cols["m_o"] // d_m)),
            pl.BlockSpec((CHUNK, LANES), lambda b, c: (row(b, c), cols["m_gates"] // LANES)),
            pl.BlockSpec((taps, 2 * d_m), const2),
            pl.BlockSpec((1, 2 * d_m), const2),
            pl.BlockSpec((1, LANES), const2),
            pl.BlockSpec((1, d_m), const2),
        ],
        out_specs=pl.BlockSpec((CHUNK, d_m), lambda b, c: (row(b, c), 0)),
        out_shape=jax.ShapeDtypeStruct((n, d_m), BF16),
        scratch_shapes=[
            pltpu.VMEM((SUBLANES + CHUNK, 2 * d_m), F32),
            pltpu.VMEM((heads, MLSTM_DIM, MLSTM_DIM), F32),
            pltpu.VMEM((SUBLANES, MLSTM_DIM), F32),
            pltpu.VMEM((SUBLANES, LANES), F32),
        ],
        compiler_params=_params(2),
        name="mlstm",
    )(proj, proj, proj, proj, conv_w, conv_b.reshape(1, -1), gb, gn_w.reshape(1, d_m))


def _s5_kernel(u_ref, t_ref, w_ref, v_ref, a_ref, y_ref, e_ref, s_ref, *, seqs, nsub):
    u = u_ref[...]
    e_ref[...] = _dot(u, w_ref[...])
    ns = a_ref.shape[1] // 2
    ar = a_ref[:, :ns]
    ai = a_ref[:, ns:]

    def body(c, carry):
        nxt = []
        for b in range(seqs):
            re, im = carry[b]
            row = pl.ds(b * nsub + c, 1)
            s_ref[row, :ns] = re
            s_ref[row, ns:] = im
            e = e_ref[row, :]
            nxt.append((ar * re - ai * im + e[:, :ns], ar * im + ai * re + e[:, ns:]))
        return tuple(nxt)

    zero = jnp.zeros((1, ns), F32)
    lax.fori_loop(0, nsub, body, tuple((zero, zero) for _ in range(seqs)), unroll=8)
    y_ref[...] = _dot(u, t_ref[...]) + _dot(s_ref[...].astype(BF16), v_ref[...])


def _s5_weights(a_re, a_im, log_step, b_re, b_im, c_re, c_im):
    hp = lax.Precision.HIGHEST
    g, p = a_re.shape
    ch = b_re.shape[-1]
    s = S5_SUB
    ar, ai = a_re.astype(F32), a_im.astype(F32)
    step = jnp.exp(log_step.astype(F32))[:, None]
    mag = jnp.exp(ar * step)
    abar_re, abar_im = mag * jnp.cos(ai * step), mag * jnp.sin(ai * step)
    xr, xi = abar_re - 1.0, abar_im
    den = ar * ar + ai * ai
    fr, fi = (xr * ar + xi * ai) / den, (xi * ar - xr * ai) / den
    br, bi = b_re.astype(F32), b_im.astype(F32)
    bbar_re = fr[..., None] * br - fi[..., None] * bi
    bbar_im = fr[..., None] * bi + fi[..., None] * br
    j = jnp.arange(s + 1, dtype=F32)[:, None, None]
    pmag = jnp.exp(ar[None] * step[None] * j)
    pr, pi = pmag * jnp.cos(ai[None] * step[None] * j), pmag * jnp.sin(ai[None] * step[None] * j)
    cr, ci = c_re.astype(F32), c_im.astype(F32)
    cp_re = cr[None] * pr[:, :, None, :] - ci[None] * pi[:, :, None, :]
    cp_im = cr[None] * pi[:, :, None, :] + ci[None] * pr[:, :, None, :]
    kern = (jnp.einsum('jgcp,gpd->jgcd', cp_re, bbar_re, precision=hp)
            - jnp.einsum('jgcp,gpd->jgcd', cp_im, bbar_im, precision=hp))
    tt = jnp.arange(s)
    lag = tt[None, :] - tt[:, None]
    toe = jnp.where((lag >= 0)[:, :, None, None, None], kern[jnp.clip(lag, 0, s)], 0.0)
    ngl = LANES // ch
    nq = g // ngl
    eye = jnp.eye(ngl, dtype=F32)
    toe = toe.reshape(s, s, nq, ngl, ch, ch)
    t_blk = toe[:, :, :, :, None, :, :] * eye[None, None, None, :, :, None, None]
    t_blk = t_blk.transpose(2, 0, 3, 6, 1, 4, 5).reshape(nq, s * LANES, s * LANES)
    prr, pir = pr[s - 1 - tt], pi[s - 1 - tt]
    w_re = prr[..., None] * bbar_re[None] - pir[..., None] * bbar_im[None]
    w_im = prr[..., None] * bbar_im[None] + pir[..., None] * bbar_re[None]

    def w_expand(w):
        w = w.reshape(s, nq, ngl, p, ch)
        w = w[:, :, :, None, :, :] * eye[None, None, :, :, None, None]
        return w.transpose(1, 0, 2, 5, 3, 4).reshape(nq, s * LANES, ngl * p)

    w_blk = jnp.concatenate([w_expand(w_re), w_expand(w_im)], axis=-1)

    def v_expand(v):
        v = v.reshape(s, nq, ngl, ch, p)
        v = v[:, :, :, None, :, :] * eye[None, None, :, :, None, None]
        return v.transpose(1, 2, 5, 0, 3, 4).reshape(nq, ngl * p, s * LANES)

    v_blk = jnp.concatenate([v_expand(cp_re[1:]), v_expand(-cp_im[1:])], axis=1)
    a_blk = jnp.concatenate([pr[s].reshape(nq, 1, ngl * p), pi[s].reshape(nq, 1, ngl * p)], axis=-1)
    return t_blk.astype(BF16), w_blk.astype(BF16), v_blk.astype(BF16), a_blk


def _s5(u_slabs, a_re, a_im, log_step, b_re, b_im, c_re, c_im, *, batch, t, seqs):
    nq, n, _ = u_slabs.shape
    s = S5_SUB
    nsub = t // s
    t_blk, w_blk, v_blk, a_blk = _s5_weights(a_re, a_im, log_step, b_re, b_im, c_re, c_im)
    kw = s * LANES
    ns2 = w_blk.shape[-1]
    rows = seqs * nsub
    u = u_slabs.reshape(nq, n // s, kw)
    once = dict(pipeline_mode=pl.Buffered(1))
    y = pl.pallas_call(
        functools.partial(_s5_kernel, seqs=seqs, nsub=nsub),
        grid=(nq, batch // seqs),
        in_specs=[
            pl.BlockSpec((None, rows, kw), lambda q, i: (q, i, 0)),
            pl.BlockSpec((None, kw, kw), lambda q, i: (q, 0, 0), **once),
            pl.BlockSpec((None, kw, ns2), lambda q, i: (q, 0, 0), **once),
            pl.BlockSpec((None, ns2, kw), lambda q, i: (q, 0, 0), **once),
            pl.BlockSpec((None, 1, ns2), lambda q, i: (q, 0, 0)),
        ],
        out_specs=pl.BlockSpec((None, rows, kw), lambda q, i: (q, i, 0)),
        out_shape=jax.ShapeDtypeStruct((nq, n // s, kw), F32),
        scratch_shapes=[pltpu.VMEM((rows, ns2), F32), pltpu.VMEM((rows, ns2), F32)],
        compiler_params=_params(2),
        name="s5_scan",
    )(u, t_blk, w_blk, v_blk, a_blk)
    return y.reshape(nq, n, LANES)


def _s5_glu_kernel(y_ref, u_ref, d_ref, w_ref, b_ref, o_ref):
    y = jnp.concatenate([y_ref[q] for q in range(y_ref.shape[0])], axis=1)
    y = y + d_ref[...] * u_ref[...]
    g = jax.nn.gelu(y)
    z = _dot(g.astype(BF16), w_ref[...]) + b_ref[...]
    o_ref[...] = (g * jax.nn.sigmoid(z)).astype(o_ref.dtype)


def _s5_glu(y, proj, cols, d_skip, glu_w, glu_b, *, tm):
    nq, n, _ = y.shape
    d_s5 = nq * LANES
    const2 = lambda i: (0, 0)
    return pl.pallas_call(
        _s5_glu_kernel,
        grid=(n // tm,),
        in_specs=[
            pl.BlockSpec((nq, tm, LANES), lambda i: (0, i, 0)),
            pl.BlockSpec((tm, d_s5), lambda i: (i, cols["s_u"] // d_s5)),
            pl.BlockSpec((1, d_s5), const2),
            pl.BlockSpec((d_s5, d_s5), const2),
            pl.BlockSpec((1, d_s5), const2),
        ],
        out_specs=pl.BlockSpec((tm, d_s5), lambda i: (i, 0)),
        out_shape=jax.ShapeDtypeStruct((n, d_s5), BF16),
        compiler_params=_params(1),
        name="s5_glu",
    )(y, proj, d_skip.reshape(1, d_s5), glu_w.astype(BF16), glu_b.reshape(1, d_s5))


def _out_proj_kernel(x_ref, a_ref, b_ref, c_ref, wa_ref, wb_ref, wc_ref, o_ref):
    acc = _dot(a_ref[...], wa_ref[...])
    acc = acc + _dot(b_ref[...], wb_ref[...])
    acc = acc + _dot(c_ref[...], wc_ref[...])
    o_ref[...] = x_ref[...] + acc


def _out_proj(x, ret, ml, s5o, w_out, *, tm):
    n, d = x.shape
    da, db, dc = ret.shape[1], ml.shape[1], s5o.shape[1]
    assert da == db and (da + db) % dc == 0
    return pl.pallas_call(
        _out_proj_kernel,
        grid=(n // tm,),
        in_specs=[
            pl.BlockSpec((tm, d), lambda i: (i, 0)),
            pl.BlockSpec((tm, da), lambda i: (i, 0)),
            pl.BlockSpec((tm, db), lambda i: (i, 0)),
            pl.BlockSpec((tm, dc), lambda i: (i, 0)),
            pl.BlockSpec((da, d), lambda i: (0, 0)),
            pl.BlockSpec((db, d), lambda i: (1, 0)),
            pl.BlockSpec((dc, d), lambda i: ((da + db) // dc, 0)),
        ],
        out_specs=pl.BlockSpec((tm, d), lambda i: (i, 0)),
        out_shape=jax.ShapeDtypeStruct((n, d), F32),
        compiler_params=_params(1),
        name="out_proj",
    )(x, ret, ml, s5o, w_out, w_out, w_out)


def _ffn_kernel(x_ref, nw_ref, wv_ref, wg_ref, cwv_ref, cwg_ref, cbv_ref, cbg_ref, wd_ref, fnw_ref,
                o_ref, h_ref, acc_ref, ubv, ubg, halo_v, halo_g, *, final_norm):
    ti = pl.program_id(1)
    j = pl.program_id(2)
    tm = x_ref.shape[0]

    @pl.when(j == 0)
    def _():
        h_ref[...] = _rms(x_ref[...], nw_ref[...]).astype(h_ref.dtype)
        acc_ref[...] = jnp.zeros_like(acc_ref)

    h = h_ref[...]

    def conv_branch(w_ref, cw_ref, cb_ref, ub, halo):
        up = _dot(h, w_ref[...])
        taps = cw_ref.shape[0]
        @pl.when(ti == 0)
        def _():
            ub[0:SUBLANES, :] = jnp.zeros((SUBLANES, ub.shape[1]), F32)

        @pl.when(ti > 0)
        def _():
            ub[0:SUBLANES, :] = halo[j]

        ub[SUBLANES:SUBLANES + tm, :] = up
        halo[j] = up[tm - SUBLANES:tm, :]
        cw = cw_ref[...]
        y = cb_ref[...] + cw[taps - 1:taps, :] * up
        for s in range(1, taps):
            y = y + cw[taps - 1 - s:taps - s, :] * ub[SUBLANES - s:SUBLANES - s + tm, :]
        return y

    val = conv_branch(wv_ref, cwv_ref, cbv_ref, ubv, halo_v)
    gate = conv_branch(wg_ref, cwg_ref, cbg_ref, ubg, halo_g)
    act = (gate * jax.nn.sigmoid(gate) * val).astype(BF16)
    acc_ref[...] += _dot(act, wd_ref[...])

    @pl.when(j == pl.num_programs(2) - 1)
    def _():
        y = x_ref[...] + acc_ref[...]
        if final_norm:
            y = _rms(y, fnw_ref[...])
        o_ref[...] = y


def _ffn(x, norm_w, w_up, conv_w, conv_b, w_down, final_w, *, batch, t, tm, tf, final_norm):
    n, d = x.shape
    d_ff = w_down.shape[0]
    nj = d_ff // tf
    nt = t // tm
    taps = conv_w.shape[0]
    row = lambda b, i, j: (b * nt + i, 0)
    const2 = lambda b, i, j: (0, 0)
    cb = conv_b.reshape(1, -1)
    return pl.pallas_call(
        functools.partial(_ffn_kernel, final_norm=final_norm),
        grid=(batch, nt, nj),
        in_specs=[
            pl.BlockSpec((tm, d), row),
            pl.BlockSpec((1, d), const2),
            pl.BlockSpec((d, tf), lambda b, i, j: (0, j)),
            pl.BlockSpec((d, tf), lambda b, i, j: (0, j + nj)),
            pl.BlockSpec((taps, tf), lambda b, i, j: (0, j)),
            pl.BlockSpec((taps, tf), lambda b, i, j: (0, j + nj)),
            pl.BlockSpec((1, tf), lambda b, i, j: (0, j)),
            pl.BlockSpec((1, tf), lambda b, i, j: (0, j + nj)),
            pl.BlockSpec((tf, d), lambda b, i, j: (j, 0)),
            pl.BlockSpec((1, d), const2),
        ],
        out_specs=pl.BlockSpec((tm, d), row),
        out_shape=jax.ShapeDtypeStruct((n, d), F32),
        scratch_shapes=[
            pltpu.VMEM((tm, d), BF16),
            pltpu.VMEM((tm, d), F32),
            pltpu.VMEM((SUBLANES + tm, tf), F32),
            pltpu.VMEM((SUBLANES + tm, tf), F32),
            pltpu.VMEM((nj, SUBLANES, tf), F32),
            pltpu.VMEM((nj, SUBLANES, tf), F32),
        ],
        compiler_params=_params(3),
        name="conv_ffn",
    )(x, norm_w.reshape(1, d), w_up, w_up, conv_w, conv_w, cb, cb, w_down, final_w.reshape(1, d))


def _proj_layout(d_ret, d_m, d_s5, heads_r, heads_m):
    d_qk = heads_r * RET_QK_DIM
    src_sizes = [("r_q", d_qk), ("r_k", d_qk), ("r_v", d_ret), ("r_g", d_ret), ("m_qk", 2 * d_m),
                 ("m_v", d_m), ("m_o", d_m), ("m_gates", 2 * heads_m), ("s_u", d_s5)]
    src, off = {}, 0
    for name, size in src_sizes:
        src[name] = (off, size)
        off += size
    order = ["m_qk", "r_v", "r_g", "m_v", "m_o", "r_q", "r_k", "m_gates", "s_u"]
    cols, pieces, off = {}, [], 0
    for name in order:
        start, size = src[name]
        align = {"m_gates": LANES, "s_u": d_s5}.get(name, size)
        pad = (-off) % align
        if pad:
            pieces.append((None, pad))
            off += pad
        cols[name] = off
        pieces.append((start, size))
        off += size
        if name == "m_gates":
            pieces.append((None, LANES - size))
            off += LANES - size
    return cols, pieces, off


def _pick(n, pref):
    return pref if n % pref == 0 else n


def kernel(x, norm1_w, w_in, mlstm_conv_w, mlstm_conv_b, mlstm_gate_b, ret_gn_w, mlstm_gn_w,
           s5_A_re, s5_A_im, s5_log_step, s5_B_re, s5_B_im, s5_C_re, s5_C_im, s5_D,
           s5_glu_w, s5_glu_b, w_out, norm2_w, ffn_w_up, ffn_conv_w, ffn_conv_b, ffn_w_down,
           final_norm_w):
    batch, t, d = x.shape
    depth = w_in.shape[0]
    n = batch * t
    d_ret, d_m, d_s5 = ret_gn_w.shape[1], mlstm_gn_w.shape[1], s5_D.shape[1]
    heads_r, heads_m = d_ret // RET_V_DIM, d_m // MLSTM_DIM
    cols, pieces, width = _proj_layout(d_ret, d_m, d_s5, heads_r, heads_m)
    tn = 1536
    width_p = -(-width // tn) * tn
    tm = _pick(t, 512)
    tf = _pick(ffn_w_down.shape[1], 512)

    xf = x.reshape(n, d)
    for l in range(depth):
        w_l = w_in[l]
        parts = [jnp.zeros((d, size), BF16) if start is None else w_l[:, start:start + size].astype(BF16)
                 for start, size in pieces]
        if width_p > width:
            parts.append(jnp.zeros((d, width_p - width), BF16))
        w_perm = jnp.concatenate(parts, axis=1)
        proj, u_slabs = _norm_matmul(xf, norm1_w[l], w_perm, tm=tm, tn=tn, d_s5=d_s5)
        ret = _retention(proj, cols, ret_gn_w[l], batch=batch, t=t)
        ml = _mlstm(proj, cols, mlstm_conv_w[l], mlstm_conv_b[l], mlstm_gate_b[l], mlstm_gn_w[l],
                    batch=batch, t=t)
        y5 = _s5(u_slabs, s5_A_re[l], s5_A_im[l], s5_log_step[l], s5_B_re[l], s5_B_im[l],
                 s5_C_re[l], s5_C_im[l], batch=batch, t=t, seqs=2 if batch % 2 == 0 else 1)
        s5o = _s5_glu(y5, proj, cols, s5_D[l], s5_glu_w[l], s5_glu_b[l], tm=tm)
        xf = _out_proj(xf, ret, ml, s5o, w_out[l].astype(BF16), tm=_pick(t, 256))
        xf = _ffn(xf, norm2_w[l], ffn_w_up[l].astype(BF16), ffn_conv_w[l], ffn_conv_b[l],
                  ffn_w_down[l].astype(BF16), final_norm_w, batch=batch, t=t, tm=tm, tf=tf,
                  final_norm=(l == depth - 1))
    return xf.reshape(batch, t, d)
```

```python
import functools

import jax
import jax.numpy as jnp
from jax import lax
from jax.experimental import pallas as pl
from jax.experimental.pallas import tpu as pltpu

RET_V_DIM = 128
RET_QK_DIM = 64
MLSTM_DIM = 128
ROPE_BASE = 10000.0
EPS = 1e-6

CHUNK = 128
S5_SUB = 16
SUBLANES = 8
LANES = 128

VMEM_LIMIT_BYTES = 56 * 1024 * 1024

F32 = jnp.float32
BF16 = jnp.bfloat16


def _params(n_grid):
    return pltpu.CompilerParams(
        dimension_semantics=("arbitrary",) * n_grid, vmem_limit_bytes=VMEM_LIMIT_BYTES)


def _rms(x, w):
    return x * lax.rsqrt(jnp.mean(x * x, axis=-1, keepdims=True) + EPS) * w


def _group_norm(o, w):
    mu = jnp.mean(o, axis=-1, keepdims=True)
    d = o - mu
    var = jnp.mean(d * d, axis=-1, keepdims=True)
    return d * lax.rsqrt(var + EPS) * w


def _dot(a, b):
    return jnp.dot(a, b, preferred_element_type=F32)


def _dot_nt(a, b):
    return lax.dot_general(a, b, (((1,), (1,)), ((), ())), preferred_element_type=F32)


def _dot_tn(a, b):
    return lax.dot_general(a, b, (((0,), (0,)), ((), ())), preferred_element_type=F32)


def _causal_conv(x, prev, cw, cb):
    taps = cw.shape[0]
    y = cb + cw[taps - 1:taps, :] * x
    row = lax.broadcasted_iota(jnp.int32, prev.shape, 0)
    for s in range(1, taps):
        sh = pltpu.roll(x, s, 0)
        head = jnp.where(row < s, pltpu.roll(prev, s, 0), sh[:SUBLANES, :])
        sh = jnp.concatenate([head, sh[SUBLANES:, :]], axis=0)
        y = y + cw[taps - 1 - s:taps - s, :] * sh
    return y


def _in_proj_kernel(x_ref, nw_ref, w_ref, wt_ref, o_ref, t_ref, u_ref, h_ref):
    j = pl.program_id(1)
    last = pl.num_programs(1) - 1

    @pl.when(j == 0)
    def _():
        h_ref[...] = _rms(x_ref[...], nw_ref[...]).astype(h_ref.dtype)

    @pl.when(j < last)
    def _():
        o_ref[...] = _dot(h_ref[...], w_ref[...]).astype(o_ref.dtype)

    @pl.when(j == last)
    def _():
        res = _dot(h_ref[...], wt_ref[...])
        t_ref[...] = res
        for q in range(u_ref.shape[0]):
            u_ref[q] = res[:, q * LANES:(q + 1) * LANES].astype(u_ref.dtype)


def _in_proj(x, norm_w, w_all, w_tail, layer, *, main, d_s5, tm, tn):
    n, d = x.shape
    nj = main // tn
    slabs = d_s5 // LANES
    wt = w_tail.shape[1]
    return pl.pallas_call(
        _in_proj_kernel,
        grid=(n // tm, nj + 1),
        in_specs=[
            pl.BlockSpec((tm, d), lambda i, j: (i, 0)),
            pl.BlockSpec((1, d), lambda i, j: (0, 0)),
            pl.BlockSpec((None, d, tn), lambda i, j: (layer, 0, jnp.minimum(j, nj - 1))),
            pl.BlockSpec((d, wt), lambda i, j: (0, 0)),
        ],
        out_specs=[
            pl.BlockSpec((tm, tn), lambda i, j: (i, jnp.minimum(j, nj - 1))),
            pl.BlockSpec((tm, wt), lambda i, j: (i, 0)),
            pl.BlockSpec((slabs, tm, LANES), lambda i, j: (0, i, 0)),
        ],
        out_shape=[
            jax.ShapeDtypeStruct((n, main), BF16),
            jax.ShapeDtypeStruct((n, wt), F32),
            jax.ShapeDtypeStruct((slabs, n, LANES), BF16),
        ],
        scratch_shapes=[pltpu.VMEM((tm, d), BF16)],
        compiler_params=_params(2),
        name="norm_in_proj",
    )(x, norm_w.reshape(1, d), w_all, w_tail)


def _retention_kernel(q_ref, k_ref, v_ref, g_ref, cos_ref, sin_ref, intra_ref, qd_ref, kd_ref,
                      cd_ref, gnw_ref, o_ref, st_ref, *, heads):
    @pl.when(pl.program_id(1) == 0)
    def _():
        st_ref[...] = jnp.zeros_like(st_ref)

    cos = cos_ref[...]
    sin = sin_ref[...]
    width = cos.shape[1]
    half = RET_QK_DIM // 2
    first_half = (lax.broadcasted_iota(jnp.int32, cos.shape, 1) % RET_QK_DIM) < half

    def rope(x):
        swapped = jnp.where(first_half, pltpu.roll(x, width - half, 1), pltpu.roll(x, half, 1))
        return x * cos + swapped * sin

    q = rope(q_ref[...].astype(F32)) * (RET_QK_DIM ** -0.5)
    k = rope(k_ref[...].astype(F32))
    for h in range(heads):
        qs = slice(h * RET_QK_DIM, (h + 1) * RET_QK_DIM)
        vs = slice(h * RET_V_DIM, (h + 1) * RET_V_DIM)
        qh = q[:, qs].astype(BF16)
        kh = k[:, qs]
        vh = v_ref[:, vs].astype(BF16)
        s = _dot_nt(qh, kh.astype(BF16)) * intra_ref[h]
        inner = _dot(s.astype(BF16), vh)
        st = st_ref[h]
        cross = _dot(qh, st.astype(BF16)) * qd_ref[h]
        st_ref[h] = st * cd_ref[h] + _dot_tn((kh * kd_ref[h]).astype(BF16), vh)
        y = _group_norm(inner + cross, gnw_ref[:, vs])
        g = g_ref[:, vs].astype(F32)
        o_ref[:, vs] = (g * jax.nn.sigmoid(g) * y).astype(o_ref.dtype)


def _retention_tables(t, heads):
    dh = RET_QK_DIM
    inv = ROPE_BASE ** (-jnp.arange(0, dh, 2, dtype=F32) / dh)
    ang = jnp.arange(t, dtype=F32)[:, None] * inv[None, :]
    cos, sin = jnp.cos(ang), jnp.sin(ang)
    cos_t = jnp.tile(jnp.concatenate([cos, cos], axis=-1), (1, heads))
    sin_t = jnp.tile(jnp.concatenate([-sin, sin], axis=-1), (1, heads))
    log_gamma = jnp.log1p(-jnp.exp2(-5.0 - jnp.arange(heads, dtype=F32)))
    idx = jnp.arange(CHUNK, dtype=F32)
    diff = idx[:, None] - idx[None, :]
    intra = jnp.where(diff[None] >= 0,
                      jnp.exp(jnp.maximum(diff, 0.0)[None] * log_gamma[:, None, None]), 0.0)
    q_decay = jnp.exp((idx[None, :] + 1.0) * log_gamma[:, None])
    k_decay = jnp.exp((CHUNK - 1.0 - idx)[None, :] * log_gamma[:, None])
    chunk_decay = jnp.exp(CHUNK * log_gamma)
    qd = jnp.broadcast_to(q_decay[:, :, None], (heads, CHUNK, RET_V_DIM))
    kd = jnp.broadcast_to(k_decay[:, :, None], (heads, CHUNK, RET_QK_DIM))
    cd = jnp.broadcast_to(chunk_decay[:, None, None], (heads, RET_QK_DIM, RET_V_DIM))
    return cos_t, sin_t, intra, qd, kd, cd


def _retention(proj, cols, gn_w, *, batch, t):
    n = proj.shape[0]
    d_ret = gn_w.shape[0]
    heads = d_ret // RET_V_DIM
    d_qk = heads * RET_QK_DIM
    nc = t // CHUNK
    cos_t, sin_t, intra, qd, kd, cd = _retention_tables(t, heads)
    row = lambda b, c: b * nc + c
    const3 = lambda b, c: (0, 0, 0)
    return pl.pallas_call(
        functools.partial(_retention_kernel, heads=heads),
        grid=(batch, nc),
        in_specs=[
            pl.BlockSpec((CHUNK, d_qk), lambda b, c: (row(b, c), cols["r_q"] // d_qk)),
            pl.BlockSpec((CHUNK, d_qk), lambda b, c: (row(b, c), cols["r_k"] // d_qk)),
            pl.BlockSpec((CHUNK, d_ret), lambda b, c: (row(b, c), cols["r_v"] // d_ret)),
            pl.BlockSpec((CHUNK, d_ret), lambda b, c: (row(b, c), cols["r_g"] // d_ret)),
            pl.BlockSpec((CHUNK, d_qk), lambda b, c: (c, 0)),
            pl.BlockSpec((CHUNK, d_qk), lambda b, c: (c, 0)),
            pl.BlockSpec(intra.shape, const3),
            pl.BlockSpec(qd.shape, const3),
            pl.BlockSpec(kd.shape, const3),
            pl.BlockSpec(cd.shape, const3),
            pl.BlockSpec((1, d_ret), lambda b, c: (0, 0)),
        ],
        out_specs=pl.BlockSpec((CHUNK, d_ret), lambda b, c: (row(b, c), 0)),
        out_shape=jax.ShapeDtypeStruct((n, d_ret), BF16),
        scratch_shapes=[pltpu.VMEM((heads, RET_QK_DIM, RET_V_DIM), F32)],
        compiler_params=_params(2),
        name="retention",
    )(proj, proj, proj, proj, cos_t, sin_t, intra, qd, kd, cd, gn_w.reshape(1, d_ret))


def _mlstm_kernel(mq_ref, mk_ref, v_ref, og_ref, gates_ref, cw_ref, cb_ref, gb_ref, gnw_ref, o_ref,
                  xprev, c_st, n_st, m_st, *, heads):
    L = CHUNK
    d = MLSTM_DIM

    @pl.when(pl.program_id(1) == 0)
    def _():
        xprev[...] = jnp.zeros_like(xprev)
        c_st[...] = jnp.zeros_like(c_st)
        n_st[...] = jnp.zeros_like(n_st)
        m_st[...] = jnp.zeros_like(m_st)

    x = jnp.concatenate([mq_ref[...], mk_ref[...]], axis=1).astype(F32)
    y = _causal_conv(x, xprev[...], cw_ref[...], cb_ref[...])
    xprev[...] = x[L - SUBLANES:L, :]
    a = y * jax.nn.sigmoid(y)
    d_m = heads * d
    q = a[:, :d_m]
    k = a[:, d_m:] * (d ** -0.5)

    gt = gates_ref[...] + gb_ref[...]
    lane = lax.broadcasted_iota(jnp.int32, gt.shape, 1)
    gate_tile = jnp.where(lane < heads, gt, jnp.where(lane < 2 * heads, jax.nn.log_sigmoid(gt), 0.0))
    r = lax.broadcasted_iota(jnp.int32, (L, L), 0)
    cidx = lax.broadcasted_iota(jnp.int32, (L, L), 1)
    causal = r >= cidx
    cum_col = jnp.dot(causal.astype(F32), gate_tile, precision=lax.Precision.HIGHEST,
                      preferred_element_type=F32)
    gate_rows = gate_tile.T
    cum_row = jnp.dot(gate_rows, (r <= cidx).astype(F32), precision=lax.Precision.HIGHEST,
                      preferred_element_type=F32)

    for h in range(heads):
        hs = slice(h * d, (h + 1) * d)
        f = heads + h
        cum_c = cum_col[:, f:f + 1]
        cum_r = cum_row[f:f + 1, :]
        ii_r = gate_rows[h:h + 1, :]
        ii_c = gate_tile[:, h:h + 1]
        m_prev = m_st[h:h + 1, 0:1]
        logw = jnp.where(causal, cum_c - cum_r + ii_r, -jnp.inf)
        inter = cum_c + m_prev
        m_t = jnp.maximum(inter, jnp.max(logw, axis=-1, keepdims=True))
        w = jnp.exp(logw - m_t)
        sc = jnp.exp(inter - m_t)
        qh = q[:, hs]
        kh = k[:, hs]
        qb = qh.astype(BF16)
        vb = v_ref[:, hs].astype(BF16)
        qk = _dot_nt(qb, kh.astype(BF16)) * w
        c_prev = c_st[h]
        n_prev = n_st[h:h + 1, :]
        num = _dot(qk.astype(BF16), vb) + sc * _dot(qb, c_prev.astype(BF16))
        den = jnp.sum(qk, axis=-1, keepdims=True) + sc * jnp.sum(qh * n_prev, axis=-1, keepdims=True)
        hv = num / jnp.maximum(jnp.abs(den), jnp.exp(-m_t))
        last = cum_c[L - 1:L, :]
        logw_end = last - cum_c + ii_c
        m_new = jnp.maximum(last + m_prev, jnp.max(logw_end, axis=0, keepdims=True))
        kw = kh * jnp.exp(logw_end - m_new)
        decay = jnp.exp(last + m_prev - m_new)
        c_st[h] = decay * c_prev + _dot_tn(kw.astype(BF16), vb)
        n_st[h:h + 1, :] = decay * n_prev + jnp.sum(kw, axis=0, keepdims=True)
        m_st[h:h + 1, :] = jnp.broadcast_to(m_new, (1, m_st.shape[1]))
        yn = _group_norm(hv, gnw_ref[:, hs])
        o_ref[:, hs] = (jax.nn.sigmoid(og_ref[:, hs].astype(F32)) * yn).astype(o_ref.dtype)


def _mlstm(proj, tail, cols, conv_w, conv_b, gate_b, gn_w, *, batch, t):
    n = proj.shape[0]
    d_m = gn_w.shape[0]
    heads = d_m // MLSTM_DIM
    nc = t // CHUNK
    taps = conv_w.shape[0]
    gb = jnp.zeros((1, LANES), F32).at[0, :2 * heads].set(gate_b)
    row = lambda b, c: b * nc + c
    const2 = lambda b, c: (0, 0)
    return pl.pallas_call(
        functools.partial(_mlstm_kernel, heads=heads),
        grid=(batch, nc),
        in_specs=[
            pl.BlockSpec((CHUNK, d_m), lambda b, c: (row(b, c), cols["m_q"] // d_m)),
            pl.BlockSpec((CHUNK, d_m), lambda b, c: (row(b, c), cols["m_k"] // d_m)),
            pl.BlockSpec((CHUNK, d_m), lambda b, c: (row(b, c), cols["m_v"] // d_m)),
            pl.BlockSpec((CHUNK, d_m), lambda b, c: (row(b, c), cols["m_o"] // d_m)),
            pl.BlockSpec((CHUNK, LANES), lambda b, c: (row(b, c), cols["t_gates"] // LANES)),
            pl.BlockSpec((taps, 2 * d_m), const2),
            pl.BlockSpec((1, 2 * d_m), const2),
            pl.BlockSpec((1, LANES), const2),
            pl.BlockSpec((1, d_m), const2),
        ],
        out_specs=pl.BlockSpec((CHUNK, d_m), lambda b, c: (row(b, c), 0)),
        out_shape=jax.ShapeDtypeStruct((n, d_m), BF16),
        scratch_shapes=[
            pltpu.VMEM((SUBLANES, 2 * d_m), F32),
            pltpu.VMEM((heads, MLSTM_DIM, MLSTM_DIM), F32),
            pltpu.VMEM((SUBLANES, MLSTM_DIM), F32),
            pltpu.VMEM((SUBLANES, LANES), F32),
        ],
        compiler_params=_params(2),
        name="mlstm",
    )(proj, proj, proj, proj, tail, conv_w, conv_b.reshape(1, -1), gb, gn_w.reshape(1, d_m))


def _s5_kernel(u_ref, t_ref, w_ref, v_ref, a_ref, y_ref, e_ref, s_ref, *, seqs, nsub):
    u = u_ref[...]
    e_ref[...] = _dot(u, w_ref[...])
    ns = a_ref.shape[1] // 2
    ar = a_ref[:, :ns]
    ai = a_ref[:, ns:]

    def body(c, carry):
        nxt = []
        for b in range(seqs):
            re, im = carry[b]
            row = pl.ds(b * nsub + c, 1)
            s_ref[row, :ns] = re
            s_ref[row, ns:] = im
            e = e_ref[row, :]
            nxt.append((ar * re - ai * im + e[:, :ns], ar * im + ai * re + e[:, ns:]))
        return tuple(nxt)

    zero = jnp.zeros((1, ns), F32)
    lax.fori_loop(0, nsub, body, tuple((zero, zero) for _ in range(seqs)), unroll=8)
    y_ref[...] = _dot(u, t_ref[...]) + _dot(s_ref[...].astype(BF16), v_ref[...])


def _scatter_cols(mats, src_to_dst, n_dst):
    g, r, c = mats.shape
    ngl = src_to_dst.shape[0]
    onehot = jax.nn.one_hot(src_to_dst, n_dst, dtype=BF16)
    m4 = mats.astype(BF16).reshape(g // ngl, ngl, r, c)
    return jnp.einsum('qgab,gbn->qgan', m4, onehot)


def _s5_weights(a_re, a_im, log_step, b_re, b_im, c_re, c_im):
    hp = lax.Precision.HIGHEST
    g, p = a_re.shape
    ch = b_re.shape[-1]
    s = S5_SUB
    ngl = LANES // ch
    nq = g // ngl
    ar, ai = a_re.astype(F32), a_im.astype(F32)
    step = jnp.exp(log_step.astype(F32))[:, None]
    mag = jnp.exp(ar * step)
    abar_re, abar_im = mag * jnp.cos(ai * step), mag * jnp.sin(ai * step)
    xr, xi = abar_re - 1.0, abar_im
    den = ar * ar + ai * ai
    fr, fi = (xr * ar + xi * ai) / den, (xi * ar - xr * ai) / den
    br, bi = b_re.astype(F32), b_im.astype(F32)
    bbar_re = fr[..., None] * br - fi[..., None] * bi
    bbar_im = fr[..., None] * bi + fi[..., None] * br
    j = jnp.arange(s + 1, dtype=F32)[:, None, None]
    pmag = jnp.exp(ar[None] * step[None] * j)
    pr, pi = pmag * jnp.cos(ai[None] * step[None] * j), pmag * jnp.sin(ai[None] * step[None] * j)
    cr, ci = c_re.astype(F32), c_im.astype(F32)
    cp_re = cr[None] * pr[:, :, None, :] - ci[None] * pi[:, :, None, :]
    cp_im = cr[None] * pi[:, :, None, :] + ci[None] * pr[:, :, None, :]
    kern = (jnp.einsum('jgcp,gpd->jgcd', cp_re, bbar_re, precision=hp)
            - jnp.einsum('jgcp,gpd->jgcd', cp_im, bbar_im, precision=hp))
    tt = jnp.arange(s)
    lag = tt[None, :] - tt[:, None]
    toe = jnp.where((lag >= 0)[:, :, None, None, None], kern[jnp.clip(lag, 0, s)], 0.0)
    t_mat = toe.transpose(2, 0, 4, 1, 3).reshape(g, s * ch, s * ch)
    prr, pir = pr[s - 1 - tt], pi[s - 1 - tt]
    w_re = prr[..., None] * bbar_re[None] - pir[..., None] * bbar_im[None]
    w_im = prr[..., None] * bbar_im[None] + pir[..., None] * bbar_re[None]
    w_cat = jnp.concatenate([w_re, w_im], axis=2).transpose(1, 0, 3, 2).reshape(g, s * ch, 2 * p)
    v_cat = jnp.concatenate([cp_re[1:], -cp_im[1:]], axis=3)
    v_cat = v_cat.transpose(1, 3, 0, 2).reshape(g, 2 * p, s * ch)

    g8 = jnp.arange(ngl)[:, None]
    tc = jnp.arange(s * ch)[None, :]
    to_lanes = (tc // ch) * LANES + g8 * ch + tc % ch
    rp = jnp.arange(2 * p)[None, :]
    to_state = (rp // p) * (ngl * p) + g8 * p + rp % p
    kw, ns2 = s * LANES, 2 * ngl * p
    t_blk = _scatter_cols(t_mat, to_lanes, kw)
    t_blk = t_blk.reshape(nq, ngl, s, ch, kw).transpose(0, 2, 1, 3, 4).reshape(nq, kw, kw)
    w_blk = _scatter_cols(w_cat, to_state, ns2)
    w_blk = w_blk.reshape(nq, ngl, s, ch, ns2).transpose(0, 2, 1, 3, 4).reshape(nq, kw, ns2)
    v_blk = _scatter_cols(v_cat, to_lanes, kw)
    v_blk = v_blk.reshape(nq, ngl, 2, p, kw).transpose(0, 2, 1, 3, 4).reshape(nq, ns2, kw)
    a_blk = jnp.concatenate([pr[s].reshape(nq, 1, ngl * p), pi[s].reshape(nq, 1, ngl * p)], axis=-1)
    return t_blk, w_blk, v_blk, a_blk


def _s5(u_slabs, a_re, a_im, log_step, b_re, b_im, c_re, c_im, *, batch, t, seqs):
    nq, n, _ = u_slabs.shape
    s = S5_SUB
    nsub = t // s
    t_blk, w_blk, v_blk, a_blk = _s5_weights(a_re, a_im, log_step, b_re, b_im, c_re, c_im)
    kw = s * LANES
    ns2 = w_blk.shape[-1]
    rows = seqs * nsub
    u = u_slabs.reshape(nq, n // s, kw)
    once = dict(pipeline_mode=pl.Buffered(1))
    y = pl.pallas_call(
        functools.partial(_s5_kernel, seqs=seqs, nsub=nsub),
        grid=(nq, batch // seqs),
        in_specs=[
            pl.BlockSpec((None, rows, kw), lambda q, i: (q, i, 0)),
            pl.BlockSpec((None, kw, kw), lambda q, i: (q, 0, 0), **once),
            pl.BlockSpec((None, kw, ns2), lambda q, i: (q, 0, 0), **once),
            pl.BlockSpec((None, ns2, kw), lambda q, i: (q, 0, 0), **once),
            pl.BlockSpec((None, 1, ns2), lambda q, i: (q, 0, 0)),
        ],
        out_specs=pl.BlockSpec((None, rows, kw), lambda q, i: (q, i, 0)),
        out_shape=jax.ShapeDtypeStruct((nq, n // s, kw), F32),
        scratch_shapes=[pltpu.VMEM((rows, ns2), F32), pltpu.VMEM((rows, ns2), F32)],
        compiler_params=_params(2),
        name="s5_scan",
    )(u, t_blk, w_blk, v_blk, a_blk)
    return y.reshape(nq, n, LANES)


def _s5_glu_kernel(y_ref, u_ref, d_ref, w_ref, b_ref, o_ref):
    y = jnp.concatenate([y_ref[q] for q in range(y_ref.shape[0])], axis=1)
    y = y + d_ref[...] * u_ref[...]
    g = jax.nn.gelu(y)
    z = _dot(g.astype(BF16), w_ref[...]) + b_ref[...]
    o_ref[...] = (g * jax.nn.sigmoid(z)).astype(o_ref.dtype)


def _s5_glu(y, tail, cols, d_skip, glu_w_all, glu_b, layer, *, tm):
    nq, n, _ = y.shape
    d_s5 = nq * LANES
    const2 = lambda i: (0, 0)
    return pl.pallas_call(
        _s5_glu_kernel,
        grid=(n // tm,),
        in_specs=[
            pl.BlockSpec((nq, tm, LANES), lambda i: (0, i, 0)),
            pl.BlockSpec((tm, d_s5), lambda i: (i, cols["t_su"] // d_s5)),
            pl.BlockSpec((1, d_s5), const2),
            pl.BlockSpec((None, d_s5, d_s5), lambda i: (layer, 0, 0)),
            pl.BlockSpec((1, d_s5), const2),
        ],
        out_specs=pl.BlockSpec((tm, d_s5), lambda i: (i, 0)),
        out_shape=jax.ShapeDtypeStruct((n, d_s5), BF16),
        compiler_params=_params(1),
        name="s5_glu",
    )(y, tail, d_skip.reshape(1, d_s5), glu_w_all, glu_b.reshape(1, d_s5))


def _out_proj_kernel(x_ref, a_ref, b_ref, c_ref, wa_ref, wb_ref, wc_ref, o_ref):
    acc = _dot(a_ref[...], wa_ref[...])
    acc = acc + _dot(b_ref[...], wb_ref[...])
    acc = acc + _dot(c_ref[...], wc_ref[...])
    o_ref[...] = x_ref[...] + acc


def _out_proj(x, ret, ml, s5o, w_out_all, layer, *, tm):
    n, d = x.shape
    da, db, dc = ret.shape[1], ml.shape[1], s5o.shape[1]
    assert da == db and (da + db) % dc == 0
    return pl.pallas_call(
        _out_proj_kernel,
        grid=(n // tm,),
        in_specs=[
            pl.BlockSpec((tm, d), lambda i: (i, 0)),
            pl.BlockSpec((tm, da), lambda i: (i, 0)),
            pl.BlockSpec((tm, db), lambda i: (i, 0)),
            pl.BlockSpec((tm, dc), lambda i: (i, 0)),
            pl.BlockSpec((None, da, d), lambda i: (layer, 0, 0)),
            pl.BlockSpec((None, db, d), lambda i: (layer, 1, 0)),
            pl.BlockSpec((None, dc, d), lambda i: (layer, (da + db) // dc, 0)),
        ],
        out_specs=pl.BlockSpec((tm, d), lambda i: (i, 0)),
        out_shape=jax.ShapeDtypeStruct((n, d), F32),
        compiler_params=_params(1),
        name="out_proj",
    )(x, ret, ml, s5o, w_out_all, w_out_all, w_out_all)


def _ffn_kernel(x_ref, nw_ref, wv_ref, wg_ref, cwv_ref, cwg_ref, cbv_ref, cbg_ref, wd_ref, fnw_ref,
                o_ref, h_ref, acc_ref, act0, act1, halo_v, halo_g, *, final_norm, nj):
    ti = pl.program_id(1)
    j = pl.program_id(2)
    tm = x_ref.shape[0]

    @pl.when(j == 0)
    def _():
        h_ref[...] = _rms(x_ref[...], nw_ref[...]).astype(h_ref.dtype)
        acc_ref[...] = jnp.zeros_like(acc_ref)

    @pl.when(jnp.logical_and(j == 0, ti == 0))
    def _():
        halo_v[...] = jnp.zeros_like(halo_v)
        halo_g[...] = jnp.zeros_like(halo_g)

    def up_phase(act_ref):
        h = h_ref[...]

        def branch(w_ref, cw_ref, cb_ref, halo):
            up = _dot(h, w_ref[...])
            prev = halo[j]
            halo[j] = up[tm - SUBLANES:tm, :]
            return _causal_conv(up, prev, cw_ref[...], cb_ref[...])

        val = branch(wv_ref, cwv_ref, cbv_ref, halo_v)
        gate = branch(wg_ref, cwg_ref, cbg_ref, halo_g)
        act_ref[...] = (gate * jax.nn.sigmoid(gate) * val).astype(act_ref.dtype)

    def down_phase(act_ref):
        acc_ref[...] += _dot(act_ref[...], wd_ref[...])

    acts = (act0, act1)

    @pl.when(j == 0)
    def _():
        up_phase(acts[0])

    for parity in (0, 1):
        @pl.when(jnp.logical_and(j % 2 == parity, jnp.logical_and(j >= 1, j < nj)))
        def _():
            up_phase(acts[parity])
            down_phase(acts[1 - parity])

    @pl.when(j == nj)
    def _():
        down_phase(acts[1 - nj % 2])
        y = x_ref[...] + acc_ref[...]
        if final_norm:
            y = _rms(y, fnw_ref[...])
        o_ref[...] = y


def _ffn(x, norm_w, w_up_all, conv_w, conv_b, w_down_all, final_w, layer, *, batch, t, tm, tf,
         final_norm):
    n, d = x.shape
    d_ff = w_down_all.shape[1]
    nj = d_ff // tf
    nt = t // tm
    taps = conv_w.shape[0]
    row = lambda b, i, j: (b * nt + i, 0)
    const2 = lambda b, i, j: (0, 0)
    up_j = lambda j: jnp.minimum(j, nj - 1)
    dn_j = lambda j: jnp.maximum(j - 1, 0)
    cb = conv_b.reshape(1, -1)
    return pl.pallas_call(
        functools.partial(_ffn_kernel, final_norm=final_norm, nj=nj),
        grid=(batch, nt, nj + 1),
        in_specs=[
            pl.BlockSpec((tm, d), row),
            pl.BlockSpec((1, d), const2),
            pl.BlockSpec((None, d, tf), lambda b, i, j: (layer, 0, up_j(j))),
            pl.BlockSpec((None, d, tf), lambda b, i, j: (layer, 0, up_j(j) + nj)),
            pl.BlockSpec((taps, tf), lambda b, i, j: (0, up_j(j))),
            pl.BlockSpec((taps, tf), lambda b, i, j: (0, up_j(j) + nj)),
            pl.BlockSpec((1, tf), lambda b, i, j: (0, up_j(j))),
            pl.BlockSpec((1, tf), lambda b, i, j: (0, up_j(j) + nj)),
            pl.BlockSpec((None, tf, d), lambda b, i, j: (layer, dn_j(j), 0)),
            pl.BlockSpec((1, d), const2),
        ],
        out_specs=pl.BlockSpec((tm, d), row),
        out_shape=jax.ShapeDtypeStruct((n, d), F32),
        scratch_shapes=[
            pltpu.VMEM((tm, d), BF16),
            pltpu.VMEM((tm, d), F32),
            pltpu.VMEM((tm, tf), BF16),
            pltpu.VMEM((tm, tf), BF16),
            pltpu.VMEM((nj, SUBLANES, tf), F32),
            pltpu.VMEM((nj, SUBLANES, tf), F32),
        ],
        compiler_params=_params(3),
        name="conv_ffn",
    )(x, norm_w.reshape(1, d), w_up_all, w_up_all, conv_w, conv_w, cb, cb, w_down_all,
      final_w.reshape(1, d))


def _pick(n, pref):
    return pref if n % pref == 0 else n


def kernel(x, norm1_w, w_in, mlstm_conv_w, mlstm_conv_b, mlstm_gate_b, ret_gn_w, mlstm_gn_w,
           s5_A_re, s5_A_im, s5_log_step, s5_B_re, s5_B_im, s5_C_re, s5_C_im, s5_D,
           s5_glu_w, s5_glu_b, w_out, norm2_w, ffn_w_up, ffn_conv_w, ffn_conv_b, ffn_w_down,
           final_norm_w):
    batch, t, d = x.shape
    depth = w_in.shape[0]
    n = batch * t
    d_ret, d_m, d_s5 = ret_gn_w.shape[1], mlstm_gn_w.shape[1], s5_D.shape[1]
    heads_r, heads_m = d_ret // RET_V_DIM, d_m // MLSTM_DIM
    d_qk = heads_r * RET_QK_DIM
    cols, off = {}, 0
    for name, size in (("r_q", d_qk), ("r_k", d_qk), ("r_v", d_ret), ("r_g", d_ret), ("m_q", d_m),
                       ("m_k", d_m), ("m_v", d_m), ("m_o", d_m), ("m_gates", 2 * heads_m), ("s_u", d_s5)):
        cols[name] = off
        off += size
    main = cols["m_gates"]
    cols["t_su"], cols["t_gates"] = 0, d_s5
    tn = main // 3
    assert main % LANES == 0 and tn % LANES == 0 and 2 * heads_m <= LANES
    tm = _pick(t, 512)
    tf = _pick(ffn_w_down.shape[1], 512)

    w_in_b = w_in.astype(BF16)
    w_out_b = w_out.astype(BF16)
    w_up_b = ffn_w_up.astype(BF16)
    w_down_b = ffn_w_down.astype(BF16)
    glu_w_b = s5_glu_w.astype(BF16)

    xf = x.reshape(n, d)
    for l in range(depth):
        w_tail = jnp.concatenate(
            [w_in[l][:, cols["s_u"]:cols["s_u"] + d_s5], w_in[l][:, main:main + 2 * heads_m],
             jnp.zeros((d, LANES - 2 * heads_m), F32)], axis=1).astype(BF16)
        proj, tail, u_slabs = _in_proj(xf, norm1_w[l], w_in_b, w_tail, l, main=main, d_s5=d_s5,
                                       tm=tm, tn=tn)
        ret = _retention(proj, cols, ret_gn_w[l], batch=batch, t=t)
        ml = _mlstm(proj, tail, cols, mlstm_conv_w[l], mlstm_conv_b[l], mlstm_gate_b[l], mlstm_gn_w[l],
                    batch=batch, t=t)
        y5 = _s5(u_slabs, s5_A_re[l], s5_A_im[l], s5_log_step[l], s5_B_re[l], s5_B_im[l],
                 s5_C_re[l], s5_C_im[l], batch=batch, t=t, seqs=2 if batch % 2 == 0 else 1)
        s5o = _s5_glu(y5, tail, cols, s5_D[l], glu_w_b, s5_glu_b[l], l, tm=tm)
        xf = _out_proj(xf, ret, ml, s5o, w_out_b, l, tm=_pick(t, 256))
        xf = _ffn(xf, norm2_w[l], w_up_b, ffn_conv_w[l], ffn_conv_b[l], w_down_b, final_norm_w, l,
                  batch=batch, t=t, tm=tm, tf=tf, final_norm=(l == depth - 1))
    return xf.reshape(batch, t, d)
```

```python
import functools

import jax
import jax.numpy as jnp
from jax import lax
from jax.experimental import pallas as pl
from jax.experimental.pallas import tpu as pltpu

RET_V_DIM = 128
RET_QK_DIM = 64
MLSTM_DIM = 128
ROPE_BASE = 10000.0
EPS = 1e-6

CHUNK = 128
S5_SUB = 16
SUBLANES = 8
LANES = 128

VMEM_LIMIT_BYTES = 56 * 1024 * 1024

F32 = jnp.float32
BF16 = jnp.bfloat16


def _params(n_grid):
    return pltpu.CompilerParams(
        dimension_semantics=("arbitrary",) * n_grid, vmem_limit_bytes=VMEM_LIMIT_BYTES)


def _rms(x, w):
    return x * lax.rsqrt(jnp.mean(x * x, axis=-1, keepdims=True) + EPS) * w


def _group_norm_heads(os, ws):
    mus = [jnp.mean(o, axis=-1, keepdims=True) for o in os]
    ds = [o - mu for o, mu in zip(os, mus)]
    vs = [jnp.mean(d * d, axis=-1, keepdims=True) for d in ds]
    return [d * lax.rsqrt(v + EPS) * w for d, v, w in zip(ds, vs, ws)]


def _dot(a, b):
    return jnp.dot(a, b, preferred_element_type=F32)


def _dot_nt(a, b):
    return lax.dot_general(a, b, (((1,), (1,)), ((), ())), preferred_element_type=F32)


def _dot_tn(a, b):
    return lax.dot_general(a, b, (((0,), (0,)), ((), ())), preferred_element_type=F32)


def _causal_conv(x, prev, cw, cb):
    taps = cw.shape[0]
    y = cb + cw[taps - 1:taps, :] * x
    row = lax.broadcasted_iota(jnp.int32, prev.shape, 0)
    for s in range(1, taps):
        sh = pltpu.roll(x, s, 0)
        head = jnp.where(row < s, pltpu.roll(prev, s, 0), sh[:SUBLANES, :])
        sh = jnp.concatenate([head, sh[SUBLANES:, :]], axis=0)
        y = y + cw[taps - 1 - s:taps - s, :] * sh
    return y


def _in_proj_kernel(x_ref, nw_ref, w_ref, wt_ref, o_ref, t_ref, h_ref):
    j = pl.program_id(1)
    last = pl.num_programs(1) - 1

    @pl.when(j == 0)
    def _():
        h_ref[...] = _rms(x_ref[...], nw_ref[...]).astype(h_ref.dtype)

    @pl.when(j < last)
    def _():
        o_ref[...] = _dot(h_ref[...], w_ref[...]).astype(o_ref.dtype)

    @pl.when(j == last)
    def _():
        t_ref[...] = _dot(h_ref[...], wt_ref[...])


def _in_proj(x, norm_w, w_all, w_tail, layer, *, main, tm, tn):
    n, d = x.shape
    nj = main // tn
    wt = w_tail.shape[1]
    return pl.pallas_call(
        _in_proj_kernel,
        grid=(n // tm, nj + 1),
        in_specs=[
            pl.BlockSpec((tm, d), lambda i, j: (i, 0)),
            pl.BlockSpec((1, d), lambda i, j: (0, 0)),
            pl.BlockSpec((None, d, tn), lambda i, j: (layer, 0, jnp.minimum(j, nj - 1))),
            pl.BlockSpec((d, wt), lambda i, j: (0, 0)),
        ],
        out_specs=[
            pl.BlockSpec((tm, tn), lambda i, j: (i, jnp.minimum(j, nj - 1))),
            pl.BlockSpec((tm, wt), lambda i, j: (i, 0)),
        ],
        out_shape=[
            jax.ShapeDtypeStruct((n, main), BF16),
            jax.ShapeDtypeStruct((n, wt), F32),
        ],
        scratch_shapes=[pltpu.VMEM((tm, d), BF16)],
        compiler_params=_params(2),
        name="norm_in_proj",
    )(x, norm_w.reshape(1, d), w_all, w_tail)


def _retention_kernel(q_ref, k_ref, v_ref, g_ref, cos_ref, sin_ref, intra_ref, qd_ref, kd_ref,
                      cd_ref, gnw_ref, o_ref, st_ref, *, heads):
    @pl.when(pl.program_id(1) == 0)
    def _():
        st_ref[...] = jnp.zeros_like(st_ref)

    cos = cos_ref[...]
    sin = sin_ref[...]
    width = cos.shape[1]
    half = RET_QK_DIM // 2
    first_half = (lax.broadcasted_iota(jnp.int32, cos.shape, 1) % RET_QK_DIM) < half

    def rope(x):
        swapped = jnp.where(first_half, pltpu.roll(x, width - half, 1), pltpu.roll(x, half, 1))
        return x * cos + swapped * sin

    q = rope(q_ref[...].astype(F32)) * (RET_QK_DIM ** -0.5)
    k = rope(k_ref[...].astype(F32))
    H = range(heads)
    qs = [slice(h * RET_QK_DIM, (h + 1) * RET_QK_DIM) for h in H]
    vs = [slice(h * RET_V_DIM, (h + 1) * RET_V_DIM) for h in H]
    qh = [q[:, qs[h]].astype(BF16) for h in H]
    kh = [k[:, qs[h]] for h in H]
    vh = [v_ref[:, vs[h]].astype(BF16) for h in H]
    st = [st_ref[h] for h in H]
    s = [_dot_nt(qh[h], kh[h].astype(BF16)) * intra_ref[h] for h in H]
    cross = [_dot(qh[h], st[h].astype(BF16)) * qd_ref[h] for h in H]
    inner = [_dot(s[h].astype(BF16), vh[h]) for h in H]
    for h in H:
        st_ref[h] = st[h] * cd_ref[h] + _dot_tn((kh[h] * kd_ref[h]).astype(BF16), vh[h])
    y = _group_norm_heads([inner[h] + cross[h] for h in H], [gnw_ref[:, vs[h]] for h in H])
    for h in H:
        g = g_ref[:, vs[h]].astype(F32)
        o_ref[:, vs[h]] = (g * jax.nn.sigmoid(g) * y[h]).astype(o_ref.dtype)


def _retention_tables(t, heads):
    dh = RET_QK_DIM
    inv = ROPE_BASE ** (-jnp.arange(0, dh, 2, dtype=F32) / dh)
    ang = jnp.arange(t, dtype=F32)[:, None] * inv[None, :]
    cos, sin = jnp.cos(ang), jnp.sin(ang)
    cos_t = jnp.tile(jnp.concatenate([cos, cos], axis=-1), (1, heads))
    sin_t = jnp.tile(jnp.concatenate([-sin, sin], axis=-1), (1, heads))
    log_gamma = jnp.log1p(-jnp.exp2(-5.0 - jnp.arange(heads, dtype=F32)))
    idx = jnp.arange(CHUNK, dtype=F32)
    diff = idx[:, None] - idx[None, :]
    intra = jnp.where(diff[None] >= 0,
                      jnp.exp(jnp.maximum(diff, 0.0)[None] * log_gamma[:, None, None]), 0.0)
    q_decay = jnp.exp((idx[None, :] + 1.0) * log_gamma[:, None])
    k_decay = jnp.exp((CHUNK - 1.0 - idx)[None, :] * log_gamma[:, None])
    chunk_decay = jnp.exp(CHUNK * log_gamma)
    qd = jnp.broadcast_to(q_decay[:, :, None], (heads, CHUNK, RET_V_DIM))
    kd = jnp.broadcast_to(k_decay[:, :, None], (heads, CHUNK, RET_QK_DIM))
    cd = jnp.broadcast_to(chunk_decay[:, None, None], (heads, RET_QK_DIM, RET_V_DIM))
    return cos_t, sin_t, intra, qd, kd, cd


def _retention(proj, cols, gn_w, *, batch, t):
    n = proj.shape[0]
    d_ret = gn_w.shape[0]
    heads = d_ret // RET_V_DIM
    d_qk = heads * RET_QK_DIM
    nc = t // CHUNK
    cos_t, sin_t, intra, qd, kd, cd = _retention_tables(t, heads)
    row = lambda b, c: b * nc + c
    const3 = lambda b, c: (0, 0, 0)
    return pl.pallas_call(
        functools.partial(_retention_kernel, heads=heads),
        grid=(batch, nc),
        in_specs=[
            pl.BlockSpec((CHUNK, d_qk), lambda b, c: (row(b, c), cols["r_q"] // d_qk)),
            pl.BlockSpec((CHUNK, d_qk), lambda b, c: (row(b, c), cols["r_k"] // d_qk)),
            pl.BlockSpec((CHUNK, d_ret), lambda b, c: (row(b, c), cols["r_v"] // d_ret)),
            pl.BlockSpec((CHUNK, d_ret), lambda b, c: (row(b, c), cols["r_g"] // d_ret)),
            pl.BlockSpec((CHUNK, d_qk), lambda b, c: (c, 0)),
            pl.BlockSpec((CHUNK, d_qk), lambda b, c: (c, 0)),
            pl.BlockSpec(intra.shape, const3),
            pl.BlockSpec(qd.shape, const3),
            pl.BlockSpec(kd.shape, const3),
            pl.BlockSpec(cd.shape, const3),
            pl.BlockSpec((1, d_ret), lambda b, c: (0, 0)),
        ],
        out_specs=pl.BlockSpec((CHUNK, d_ret), lambda b, c: (row(b, c), 0)),
        out_shape=jax.ShapeDtypeStruct((n, d_ret), BF16),
        scratch_shapes=[pltpu.VMEM((heads, RET_QK_DIM, RET_V_DIM), F32)],
        compiler_params=_params(2),
        name="retention",
    )(proj, proj, proj, proj, cos_t, sin_t, intra, qd, kd, cd, gn_w.reshape(1, d_ret))


def _mlstm_kernel(mq_ref, mk_ref, v_ref, og_ref, gates_ref, cw_ref, cb_ref, gb_ref, gnw_ref, o_ref,
                  xprev, c_st, n_st, m_st, *, heads):
    L = CHUNK
    d = MLSTM_DIM

    @pl.when(pl.program_id(1) == 0)
    def _():
        xprev[...] = jnp.zeros_like(xprev)
        c_st[...] = jnp.zeros_like(c_st)
        n_st[...] = jnp.zeros_like(n_st)
        m_st[...] = jnp.zeros_like(m_st)

    x = jnp.concatenate([mq_ref[...], mk_ref[...]], axis=1).astype(F32)
    y = _causal_conv(x, xprev[...], cw_ref[...], cb_ref[...])
    xprev[...] = x[L - SUBLANES:L, :]
    a = y * jax.nn.sigmoid(y)
    d_m = heads * d
    q = a[:, :d_m]
    k = a[:, d_m:] * (d ** -0.5)

    gt = gates_ref[...] + gb_ref[...]
    lane = lax.broadcasted_iota(jnp.int32, gt.shape, 1)
    gate_tile = jnp.where(lane < heads, gt, jnp.where(lane < 2 * heads, jax.nn.log_sigmoid(gt), 0.0))
    r = lax.broadcasted_iota(jnp.int32, (L, L), 0)
    cidx = lax.broadcasted_iota(jnp.int32, (L, L), 1)
    causal = r >= cidx
    cum_col = jnp.dot(causal.astype(F32), gate_tile, precision=lax.Precision.HIGHEST,
                      preferred_element_type=F32)
    gate_rows = gate_tile.T
    cum_row = jnp.dot(gate_rows, (r <= cidx).astype(F32), precision=lax.Precision.HIGHEST,
                      preferred_element_type=F32)

    n_all = n_st[...]
    m_all = m_st[...]
    H = range(heads)
    hs = [slice(h * d, (h + 1) * d) for h in H]
    cum_c = [cum_col[:, heads + h:heads + h + 1] for h in H]
    cum_r = [cum_row[heads + h:heads + h + 1, :] for h in H]
    ii_r = [gate_rows[h:h + 1, :] for h in H]
    ii_c = [gate_tile[:, h:h + 1] for h in H]
    m_prev = [m_all[h:h + 1, 0:1] for h in H]
    n_prev = [n_all[h:h + 1, :] for h in H]
    c_prev = [c_st[h] for h in H]
    qh = [q[:, hs[h]] for h in H]
    kh = [k[:, hs[h]] for h in H]
    qb = [qh[h].astype(BF16) for h in H]
    vb = [v_ref[:, hs[h]].astype(BF16) for h in H]
    logw = [jnp.where(causal, cum_c[h] - cum_r[h] + ii_r[h], -jnp.inf) for h in H]
    inter = [cum_c[h] + m_prev[h] for h in H]
    m_t = [jnp.maximum(inter[h], jnp.max(logw[h], axis=-1, keepdims=True)) for h in H]
    w = [jnp.exp(logw[h] - m_t[h]) for h in H]
    sc = [jnp.exp(inter[h] - m_t[h]) for h in H]
    qk = [_dot_nt(qb[h], kh[h].astype(BF16)) * w[h] for h in H]
    num = [_dot(qk[h].astype(BF16), vb[h]) + sc[h] * _dot(qb[h], c_prev[h].astype(BF16)) for h in H]
    den = [jnp.sum(qk[h], axis=-1, keepdims=True)
           + sc[h] * jnp.sum(qh[h] * n_prev[h], axis=-1, keepdims=True) for h in H]
    hv = [num[h] / jnp.maximum(jnp.abs(den[h]), jnp.exp(-m_t[h])) for h in H]
    last = [cum_c[h][L - 1:L, :] for h in H]
    logw_end = [last[h] - cum_c[h] + ii_c[h] for h in H]
    m_new = [jnp.maximum(last[h] + m_prev[h], jnp.max(logw_end[h], axis=0, keepdims=True)) for h in H]
    kw = [kh[h] * jnp.exp(logw_end[h] - m_new[h]) for h in H]
    decay = [jnp.exp(last[h] + m_prev[h] - m_new[h]) for h in H]
    for h in H:
        c_st[h] = decay[h] * c_prev[h] + _dot_tn(kw[h].astype(BF16), vb[h])
    n_st[0:heads, :] = jnp.concatenate(
        [decay[h] * n_prev[h] + jnp.sum(kw[h], axis=0, keepdims=True) for h in H], axis=0)
    m_st[0:heads, :] = jnp.concatenate(
        [jnp.broadcast_to(m_new[h], (1, m_st.shape[1])) for h in H], axis=0)
    yn = _group_norm_heads(hv, [gnw_ref[:, hs[h]] for h in H])
    for h in H:
        o_ref[:, hs[h]] = (jax.nn.sigmoid(og_ref[:, hs[h]].astype(F32)) * yn[h]).astype(o_ref.dtype)


def _mlstm(proj, tail, cols, conv_w, conv_b, gate_b, gn_w, *, batch, t):
    n = proj.shape[0]
    d_m = gn_w.shape[0]
    heads = d_m // MLSTM_DIM
    nc = t // CHUNK
    taps = conv_w.shape[0]
    gb = jnp.zeros((1, LANES), F32).at[0, :2 * heads].set(gate_b)
    row = lambda b, c: b * nc + c
    const2 = lambda b, c: (0, 0)
    return pl.pallas_call(
        functools.partial(_mlstm_kernel, heads=heads),
        grid=(batch, nc),
        in_specs=[
            pl.BlockSpec((CHUNK, d_m), lambda b, c: (row(b, c), cols["m_q"] // d_m)),
            pl.BlockSpec((CHUNK, d_m), lambda b, c: (row(b, c), cols["m_k"] // d_m)),
            pl.BlockSpec((CHUNK, d_m), lambda b, c: (row(b, c), cols["m_v"] // d_m)),
            pl.BlockSpec((CHUNK, d_m), lambda b, c: (row(b, c), cols["m_o"] // d_m)),
            pl.BlockSpec((CHUNK, LANES), lambda b, c: (row(b, c), cols["t_gates"] // LANES)),
            pl.BlockSpec((taps, 2 * d_m), const2),
            pl.BlockSpec((1, 2 * d_m), const2),
            pl.BlockSpec((1, LANES), const2),
            pl.BlockSpec((1, d_m), const2),
        ],
        out_specs=pl.BlockSpec((CHUNK, d_m), lambda b, c: (row(b, c), 0)),
        out_shape=jax.ShapeDtypeStruct((n, d_m), BF16),
        scratch_shapes=[
            pltpu.VMEM((SUBLANES, 2 * d_m), F32),
            pltpu.VMEM((heads, MLSTM_DIM, MLSTM_DIM), F32),
            pltpu.VMEM((SUBLANES, MLSTM_DIM), F32),
            pltpu.VMEM((SUBLANES, LANES), F32),
        ],
        compiler_params=_params(2),
        name="mlstm",
    )(proj, proj, proj, proj, tail, conv_w, conv_b.reshape(1, -1), gb, gn_w.reshape(1, d_m))


def _s5_kernel(u_ref, bd_ref, w_ref, v_ref, a_ref, y_ref, t_scr, e_ref, s_ref, *, seqs, nsub):
    s = S5_SUB

    @pl.when(pl.program_id(1) == 0)
    def _():
        zero = jnp.zeros((LANES, LANES), t_scr.dtype)
        for ti in range(s):
            for to in range(s):
                t_scr[ti * LANES:(ti + 1) * LANES, to * LANES:(to + 1) * LANES] = (
                    bd_ref[to - ti] if to >= ti else zero)

    rows = seqs * nsub
    u = jnp.concatenate([u_ref[pl.ds(ti, rows, stride=s), :] for ti in range(s)], axis=1).astype(BF16)
    e_ref[...] = _dot(u, w_ref[...])
    ns = a_ref.shape[1] // 2
    ar = a_ref[:, :ns]
    ai = a_ref[:, ns:]

    def body(c, carry):
        nxt = []
        for b in range(seqs):
            re, im = carry[b]
            row = pl.ds(b * nsub + c, 1)
            s_ref[row, :ns] = re
            s_ref[row, ns:] = im
            e = e_ref[row, :]
            nxt.append((ar * re - ai * im + e[:, :ns], ar * im + ai * re + e[:, ns:]))
        return tuple(nxt)

    zero = jnp.zeros((1, ns), F32)
    lax.fori_loop(0, nsub, body, tuple((zero, zero) for _ in range(seqs)), unroll=8)
    y = _dot(u, t_scr[...]) + _dot(s_ref[...].astype(BF16), v_ref[...])
    for to in range(s):
        y_ref[pl.ds(to, rows, stride=s), :] = y[:, to * LANES:(to + 1) * LANES]


def _s5_weights(a_re, a_im, log_step, b_re, b_im, c_re, c_im):
    hp = lax.Precision.HIGHEST
    g, p = a_re.shape
    ch = b_re.shape[-1]
    s = S5_SUB
    ngl = LANES // ch
    nq = g // ngl
    ar, ai = a_re.astype(F32), a_im.astype(F32)
    step = jnp.exp(log_step.astype(F32))[:, None]
    mag = jnp.exp(ar * step)
    abar_re, abar_im = mag * jnp.cos(ai * step), mag * jnp.sin(ai * step)
    xr, xi = abar_re - 1.0, abar_im
    den = ar * ar + ai * ai
    fr, fi = (xr * ar + xi * ai) / den, (xi * ar - xr * ai) / den
    br, bi = b_re.astype(F32), b_im.astype(F32)
    bbar_re = fr[..., None] * br - fi[..., None] * bi
    bbar_im = fr[..., None] * bi + fi[..., None] * br
    j = jnp.arange(s + 1, dtype=F32)[:, None, None]
    pmag = jnp.exp(ar[None] * step[None] * j)
    pr, pi = pmag * jnp.cos(ai[None] * step[None] * j), pmag * jnp.sin(ai[None] * step[None] * j)
    cr, ci = c_re.astype(F32), c_im.astype(F32)
    cp_re = cr[None] * pr[:, :, None, :] - ci[None] * pi[:, :, None, :]
    cp_im = cr[None] * pi[:, :, None, :] + ci[None] * pr[:, :, None, :]
    kern = (jnp.einsum('jgcp,gpd->jgcd', cp_re, bbar_re, precision=hp)
            - jnp.einsum('jgcp,gpd->jgcd', cp_im, bbar_im, precision=hp))
    eye = jnp.eye(ngl, dtype=F32)
    bd = kern.reshape(s + 1, nq, ngl, ch, ch).transpose(1, 0, 2, 4, 3)
    bd = bd[:, :, :, :, None, :] * eye[None, None, :, None, :, None]
    bd = bd.reshape(nq, s + 1, LANES, LANES)
    tt = jnp.arange(s)
    prr, pir = pr[s - 1 - tt], pi[s - 1 - tt]
    w_re = prr[..., None] * bbar_re[None] - pir[..., None] * bbar_im[None]
    w_im = prr[..., None] * bbar_im[None] + pir[..., None] * bbar_re[None]

    def w_expand(w):
        w = w.reshape(s, nq, ngl, p, ch).transpose(1, 0, 2, 4, 3)
        return w[:, :, :, :, None, :] * eye[None, None, :, None, :, None]

    w_blk = jnp.stack([w_expand(w_re), w_expand(w_im)], axis=4)
    w_blk = w_blk.reshape(nq, s * LANES, 2 * ngl * p)

    def v_expand(v):
        v = v.reshape(s, nq, ngl, ch, p).transpose(1, 2, 4, 0, 3)
        return v[:, :, :, :, None, :] * eye[None, :, None, None, :, None]

    v_blk = jnp.stack([v_expand(cp_re[1:]), v_expand(-cp_im[1:])], axis=1)
    v_blk = v_blk.reshape(nq, 2 * ngl * p, s * LANES)
    a_blk = jnp.concatenate([pr[s].reshape(nq, 1, ngl * p), pi[s].reshape(nq, 1, ngl * p)], axis=-1)
    return bd.astype(BF16), w_blk.astype(BF16), v_blk.astype(BF16), a_blk


def _s5(tail, cols, a_re, a_im, log_step, b_re, b_im, c_re, c_im, *, batch, t, seqs):
    n = tail.shape[0]
    s = S5_SUB
    nsub = t // s
    bd, w_blk, v_blk, a_blk = _s5_weights(a_re, a_im, log_step, b_re, b_im, c_re, c_im)
    nq = bd.shape[0]
    kw = s * LANES
    ns2 = w_blk.shape[-1]
    rows = seqs * nsub
    c0 = cols["t_su"] // LANES
    once = dict(pipeline_mode=pl.Buffered(1))
    return pl.pallas_call(
        functools.partial(_s5_kernel, seqs=seqs, nsub=nsub),
        grid=(nq, batch // seqs),
        in_specs=[
            pl.BlockSpec((seqs * t, LANES), lambda q, i: (i, c0 + q)),
            pl.BlockSpec((None, s + 1, LANES, LANES), lambda q, i: (q, 0, 0, 0)),
            pl.BlockSpec((None, kw, ns2), lambda q, i: (q, 0, 0), **once),
            pl.BlockSpec((None, ns2, kw), lambda q, i: (q, 0, 0), **once),
            pl.BlockSpec((None, 1, ns2), lambda q, i: (q, 0, 0)),
        ],
        out_specs=pl.BlockSpec((seqs * t, LANES), lambda q, i: (i, q)),
        out_shape=jax.ShapeDtypeStruct((n, nq * LANES), F32),
        scratch_shapes=[pltpu.VMEM((kw, kw), BF16), pltpu.VMEM((rows, ns2), F32),
                        pltpu.VMEM((rows, ns2), F32)],
        compiler_params=_params(2),
        name="s5_scan",
    )(tail, bd, w_blk, v_blk, a_blk)


def _s5_glu_kernel(y_ref, u_ref, d_ref, w_ref, b_ref, o_ref):
    y = y_ref[...]
    y = y + d_ref[...] * u_ref[...]
    g = jax.nn.gelu(y)
    z = _dot(g.astype(BF16), w_ref[...]) + b_ref[...]
    o_ref[...] = (g * jax.nn.sigmoid(z)).astype(o_ref.dtype)


def _s5_glu(y, tail, cols, d_skip, glu_w_all, glu_b, layer, *, tm):
    n, d_s5 = y.shape
    const2 = lambda i: (0, 0)
    return pl.pallas_call(
        _s5_glu_kernel,
        grid=(n // tm,),
        in_specs=[
            pl.BlockSpec((tm, d_s5), lambda i: (i, 0)),
            pl.BlockSpec((tm, d_s5), lambda i: (i, cols["t_su"] // d_s5)),
            pl.BlockSpec((1, d_s5), const2),
            pl.BlockSpec((None, d_s5, d_s5), lambda i: (layer, 0, 0)),
            pl.BlockSpec((1, d_s5), const2),
        ],
        out_specs=pl.BlockSpec((tm, d_s5), lambda i: (i, 0)),
        out_shape=jax.ShapeDtypeStruct((n, d_s5), BF16),
        compiler_params=_params(1),
        name="s5_glu",
    )(y, tail, d_skip.reshape(1, d_s5), glu_w_all, glu_b.reshape(1, d_s5))


def _out_proj_kernel(x_ref, a_ref, b_ref, c_ref, wa_ref, wb_ref, wc_ref, o_ref):
    acc = _dot(a_ref[...], wa_ref[...])
    acc = acc + _dot(b_ref[...], wb_ref[...])
    acc = acc + _dot(c_ref[...], wc_ref[...])
    o_ref[...] = x_ref[...] + acc


def _out_proj(x, ret, ml, s5o, w_out_all, layer, *, tm):
    n, d = x.shape
    da, db, dc = ret.shape[1], ml.shape[1], s5o.shape[1]
    assert da == db and (da + db) % dc == 0
    return pl.pallas_call(
        _out_proj_kernel,
        grid=(n // tm,),
        in_specs=[
            pl.BlockSpec((tm, d), lambda i: (i, 0)),
            pl.BlockSpec((tm, da), lambda i: (i, 0)),
            pl.BlockSpec((tm, db), lambda i: (i, 0)),
            pl.BlockSpec((tm, dc), lambda i: (i, 0)),
            pl.BlockSpec((None, da, d), lambda i: (layer, 0, 0)),
            pl.BlockSpec((None, db, d), lambda i: (layer, 1, 0)),
            pl.BlockSpec((None, dc, d), lambda i: (layer, (da + db) // dc, 0)),
        ],
        out_specs=pl.BlockSpec((tm, d), lambda i: (i, 0)),
        out_shape=jax.ShapeDtypeStruct((n, d), F32),
        compiler_params=_params(1),
        name="out_proj",
    )(x, ret, ml, s5o, w_out_all, w_out_all, w_out_all)


def _ffn_kernel(x_ref, nw_ref, wv_ref, wg_ref, cwv_ref, cwg_ref, cbv_ref, cbg_ref, wd_ref, fnw_ref,
                o_ref, h_ref, acc_ref, act0, act1, halo_v, halo_g, *, final_norm, nj):
    ti = pl.program_id(1)
    j = pl.program_id(2)
    tm = x_ref.shape[0]

    @pl.when(j == 0)
    def _():
        h_ref[...] = _rms(x_ref[...], nw_ref[...]).astype(h_ref.dtype)
        acc_ref[...] = jnp.zeros_like(acc_ref)

    @pl.when(jnp.logical_and(j == 0, ti == 0))
    def _():
        halo_v[...] = jnp.zeros_like(halo_v)
        halo_g[...] = jnp.zeros_like(halo_g)

    def up_phase(act_ref):
        h = h_ref[...]

        def branch(w_ref, cw_ref, cb_ref, halo):
            up = _dot(h, w_ref[...])
            prev = halo[j]
            halo[j] = up[tm - SUBLANES:tm, :]
            return _causal_conv(up, prev, cw_ref[...], cb_ref[...])

        val = branch(wv_ref, cwv_ref, cbv_ref, halo_v)
        gate = branch(wg_ref, cwg_ref, cbg_ref, halo_g)
        act_ref[...] = (gate * jax.nn.sigmoid(gate) * val).astype(act_ref.dtype)

    def down_phase(act_ref):
        acc_ref[...] += _dot(act_ref[...], wd_ref[...])

    acts = (act0, act1)

    @pl.when(j == 0)
    def _():
        up_phase(acts[0])

    for parity in (0, 1):
        @pl.when(jnp.logical_and(j % 2 == parity, jnp.logical_and(j >= 1, j < nj)))
        def _():
            up_phase(acts[parity])
            down_phase(acts[1 - parity])

    @pl.when(j == nj)
    def _():
        down_phase(acts[1 - nj % 2])
        y = x_ref[...] + acc_ref[...]
        if final_norm:
            y = _rms(y, fnw_ref[...])
        o_ref[...] = y


def _ffn(x, norm_w, w_up_all, conv_w, conv_b, w_down_all, final_w, layer, *, batch, t, tm, tf,
         final_norm):
    n, d = x.shape
    d_ff = w_down_all.shape[1]
    nj = d_ff // tf
    nt = t // tm
    taps = conv_w.shape[0]
    row = lambda b, i, j: (b * nt + i, 0)
    const2 = lambda b, i, j: (0, 0)
    up_j = lambda j: jnp.minimum(j, nj - 1)
    dn_j = lambda j: jnp.maximum(j - 1, 0)
    cb = conv_b.reshape(1, -1)
    return pl.pallas_call(
        functools.partial(_ffn_kernel, final_norm=final_norm, nj=nj),
        grid=(batch, nt, nj + 1),
        in_specs=[
            pl.BlockSpec((tm, d), row),
            pl.BlockSpec((1, d), const2),
            pl.BlockSpec((None, d, tf), lambda b, i, j: (layer, 0, up_j(j))),
            pl.BlockSpec((None, d, tf), lambda b, i, j: (layer, 0, up_j(j) + nj)),
            pl.BlockSpec((taps, tf), lambda b, i, j: (0, up_j(j))),
            pl.BlockSpec((taps, tf), lambda b, i, j: (0, up_j(j) + nj)),
            pl.BlockSpec((1, tf), lambda b, i, j: (0, up_j(j))),
            pl.BlockSpec((1, tf), lambda b, i, j: (0, up_j(j) + nj)),
            pl.BlockSpec((None, tf, d), lambda b, i, j: (layer, dn_j(j), 0)),
            pl.BlockSpec((1, d), const2),
        ],
        out_specs=pl.BlockSpec((tm, d), row),
        out_shape=jax.ShapeDtypeStruct((n, d), F32),
        scratch_shapes=[
            pltpu.VMEM((tm, d), BF16),
            pltpu.VMEM((tm, d), F32),
            pltpu.VMEM((tm, tf), BF16),
            pltpu.VMEM((tm, tf), BF16),
            pltpu.VMEM((nj, SUBLANES, tf), F32),
            pltpu.VMEM((nj, SUBLANES, tf), F32),
        ],
        compiler_params=_params(3),
        name="conv_ffn",
    )(x, norm_w.reshape(1, d), w_up_all, w_up_all, conv_w, conv_w, cb, cb, w_down_all,
      final_w.reshape(1, d))


def _pick(n, pref):
    return pref if n % pref == 0 else n


def kernel(x, norm1_w, w_in, mlstm_conv_w, mlstm_conv_b, mlstm_gate_b, ret_gn_w, mlstm_gn_w,
           s5_A_re, s5_A_im, s5_log_step, s5_B_re, s5_B_im, s5_C_re, s5_C_im, s5_D,
           s5_glu_w, s5_glu_b, w_out, norm2_w, ffn_w_up, ffn_conv_w, ffn_conv_b, ffn_w_down,
           final_norm_w):
    batch, t, d = x.shape
    depth = w_in.shape[0]
    n = batch * t
    d_ret, d_m, d_s5 = ret_gn_w.shape[1], mlstm_gn_w.shape[1], s5_D.shape[1]
    heads_r, heads_m = d_ret // RET_V_DIM, d_m // MLSTM_DIM
    d_qk = heads_r * RET_QK_DIM
    cols, off = {}, 0
    for name, size in (("r_q", d_qk), ("r_k", d_qk), ("r_v", d_ret), ("r_g", d_ret), ("m_q", d_m),
                       ("m_k", d_m), ("m_v", d_m), ("m_o", d_m), ("m_gates", 2 * heads_m), ("s_u", d_s5)):
        cols[name] = off
        off += size
    main = cols["m_gates"]
    cols["t_su"], cols["t_gates"] = 0, d_s5
    tn = main // 6
    assert main % LANES == 0 and tn % LANES == 0 and 2 * heads_m <= LANES
    tm = _pick(t, 512)
    tf = _pick(ffn_w_down.shape[1], 512)

    w_in_b = w_in.astype(BF16)
    w_out_b = w_out.astype(BF16)
    w_up_b = ffn_w_up.astype(BF16)
    w_down_b = ffn_w_down.astype(BF16)
    glu_w_b = s5_glu_w.astype(BF16)

    xf = x.reshape(n, d)
    for l in range(depth):
        w_tail = jnp.concatenate(
            [w_in[l][:, cols["s_u"]:cols["s_u"] + d_s5], w_in[l][:, main:main + 2 * heads_m],
             jnp.zeros((d, LANES - 2 * heads_m), F32)], axis=1).astype(BF16)
        proj, tail = _in_proj(xf, norm1_w[l], w_in_b, w_tail, l, main=main, tm=_pick(t, 1024), tn=tn)
        ret = _retention(proj, cols, ret_gn_w[l], batch=batch, t=t)
        ml = _mlstm(proj, tail, cols, mlstm_conv_w[l], mlstm_conv_b[l], mlstm_gate_b[l], mlstm_gn_w[l],
                    batch=batch, t=t)
        y5 = _s5(tail, cols, s5_A_re[l], s5_A_im[l], s5_log_step[l], s5_B_re[l], s5_B_im[l],
                 s5_C_re[l], s5_C_im[l], batch=batch, t=t, seqs=2 if batch % 2 == 0 else 1)
        s5o = _s5_glu(y5, tail, cols, s5_D[l], glu_w_b, s5_glu_b[l], l, tm=tm)
        xf = _out_proj(xf, ret, ml, s5o, w_out_b, l, tm=_pick(t, 256))
        xf = _ffn(xf, norm2_w[l], w_up_b, ffn_conv_w[l], ffn_conv_b[l], w_down_b, final_norm_w, l,
                  batch=batch, t=t, tm=tm, tf=tf, final_norm=(l == depth - 1))
    return xf.reshape(batch, t, d)
```

```python
import functools

import jax
import jax.numpy as jnp
from jax import lax
from jax.experimental import pallas as pl
from jax.experimental.pallas import tpu as pltpu

RET_V_DIM = 128
RET_QK_DIM = 64
MLSTM_DIM = 128
ROPE_BASE = 10000.0
EPS = 1e-6

CHUNK = 128
S5_SUB = 16
SUBLANES = 8
LANES = 128

VMEM_LIMIT_BYTES = 56 * 1024 * 1024

F32 = jnp.float32
BF16 = jnp.bfloat16


def _params(n_grid):
    return pltpu.CompilerParams(
        dimension_semantics=("arbitrary",) * n_grid, vmem_limit_bytes=VMEM_LIMIT_BYTES)


def _rms(x, w):
    return x * lax.rsqrt(jnp.mean(x * x, axis=-1, keepdims=True) + EPS) * w


def _group_norm_heads(os, ws):
    mus = [jnp.mean(o, axis=-1, keepdims=True) for o in os]
    ds = [o - mu for o, mu in zip(os, mus)]
    vs = [jnp.mean(d * d, axis=-1, keepdims=True) for d in ds]
    return [d * lax.rsqrt(v + EPS) * w for d, v, w in zip(ds, vs, ws)]


def _dot(a, b):
    return jnp.dot(a, b, preferred_element_type=F32)


def _dot_nt(a, b):
    return lax.dot_general(a, b, (((1,), (1,)), ((), ())), preferred_element_type=F32)


def _dot_tn(a, b):
    return lax.dot_general(a, b, (((0,), (0,)), ((), ())), preferred_element_type=F32)


def _causal_conv(x, prev, cw, cb):
    taps = cw.shape[0]
    y = cb + cw[taps - 1:taps, :] * x
    row = lax.broadcasted_iota(jnp.int32, prev.shape, 0)
    for s in range(1, taps):
        sh = pltpu.roll(x, s, 0)
        head = jnp.where(row < s, pltpu.roll(prev, s, 0), sh[:SUBLANES, :])
        sh = jnp.concatenate([head, sh[SUBLANES:, :]], axis=0)
        y = y + cw[taps - 1 - s:taps - s, :] * sh
    return y


def _in_proj_kernel(x_ref, nw_ref, w_ref, wt_ref, o_ref, t_ref, h_ref):
    j = pl.program_id(1)
    last = pl.num_programs(1) - 1

    @pl.when(j == 0)
    def _():
        h_ref[...] = _rms(x_ref[...], nw_ref[...]).astype(h_ref.dtype)

    @pl.when(j < last)
    def _():
        o_ref[...] = _dot(h_ref[...], w_ref[...]).astype(o_ref.dtype)

    @pl.when(j == last)
    def _():
        t_ref[...] = _dot(h_ref[...], wt_ref[...])


def _in_proj(x, norm_w, w_all, w_tail, layer, *, main, tm, tn):
    n, d = x.shape
    nj = main // tn
    wt = w_tail.shape[1]
    return pl.pallas_call(
        _in_proj_kernel,
        grid=(n // tm, nj + 1),
        in_specs=[
            pl.BlockSpec((tm, d), lambda i, j: (i, 0)),
            pl.BlockSpec((1, d), lambda i, j: (0, 0)),
            pl.BlockSpec((None, d, tn), lambda i, j: (layer, 0, jnp.minimum(j, nj - 1))),
            pl.BlockSpec((d, wt), lambda i, j: (0, 0)),
        ],
        out_specs=[
            pl.BlockSpec((tm, tn), lambda i, j: (i, jnp.minimum(j, nj - 1))),
            pl.BlockSpec((tm, wt), lambda i, j: (i, 0)),
        ],
        out_shape=[
            jax.ShapeDtypeStruct((n, main), BF16),
            jax.ShapeDtypeStruct((n, wt), F32),
        ],
        scratch_shapes=[pltpu.VMEM((tm, d), BF16)],
        compiler_params=_params(2),
        name="norm_in_proj",
    )(x, norm_w.reshape(1, d), w_all, w_tail)


def _retention_kernel(q_ref, k_ref, v_ref, g_ref, cos_ref, sin_ref, intra_ref, qd_ref, kd_ref,
                      cd_ref, gnw_ref, o_ref, st_ref, *, heads):
    @pl.when(pl.program_id(1) == 0)
    def _():
        st_ref[...] = jnp.zeros_like(st_ref)

    cos = cos_ref[...]
    sin = sin_ref[...]
    width = cos.shape[1]
    half = RET_QK_DIM // 2
    first_half = (lax.broadcasted_iota(jnp.int32, cos.shape, 1) % RET_QK_DIM) < half

    def rope(x):
        swapped = jnp.where(first_half, pltpu.roll(x, width - half, 1), pltpu.roll(x, half, 1))
        return x * cos + swapped * sin

    q = rope(q_ref[...].astype(F32)) * (RET_QK_DIM ** -0.5)
    k = rope(k_ref[...].astype(F32))
    H = range(heads)
    qs = [slice(h * RET_QK_DIM, (h + 1) * RET_QK_DIM) for h in H]
    vs = [slice(h * RET_V_DIM, (h + 1) * RET_V_DIM) for h in H]
    qh = [q[:, qs[h]].astype(BF16) for h in H]
    kh = [k[:, qs[h]] for h in H]
    vh = [v_ref[:, vs[h]].astype(BF16) for h in H]
    st = [st_ref[h] for h in H]
    s = [_dot_nt(qh[h], kh[h].astype(BF16)) * intra_ref[h] for h in H]
    cross = [_dot(qh[h], st[h].astype(BF16)) * qd_ref[h] for h in H]
    inner = [_dot(s[h].astype(BF16), vh[h]) for h in H]
    for h in H:
        st_ref[h] = st[h] * cd_ref[h] + _dot_tn((kh[h] * kd_ref[h]).astype(BF16), vh[h])
    y = _group_norm_heads([inner[h] + cross[h] for h in H], [gnw_ref[:, vs[h]] for h in H])
    for h in H:
        g = g_ref[:, vs[h]].astype(F32)
        o_ref[:, vs[h]] = (g * jax.nn.sigmoid(g) * y[h]).astype(o_ref.dtype)


def _retention_tables(t, heads):
    dh = RET_QK_DIM
    inv = ROPE_BASE ** (-jnp.arange(0, dh, 2, dtype=F32) / dh)
    ang = jnp.arange(t, dtype=F32)[:, None] * inv[None, :]
    cos, sin = jnp.cos(ang), jnp.sin(ang)
    cos_t = jnp.tile(jnp.concatenate([cos, cos], axis=-1), (1, heads))
    sin_t = jnp.tile(jnp.concatenate([-sin, sin], axis=-1), (1, heads))
    log_gamma = jnp.log1p(-jnp.exp2(-5.0 - jnp.arange(heads, dtype=F32)))
    idx = jnp.arange(CHUNK, dtype=F32)
    diff = idx[:, None] - idx[None, :]
    intra = jnp.where(diff[None] >= 0,
                      jnp.exp(jnp.maximum(diff, 0.0)[None] * log_gamma[:, None, None]), 0.0)
    q_decay = jnp.exp((idx[None, :] + 1.0) * log_gamma[:, None])
    k_decay = jnp.exp((CHUNK - 1.0 - idx)[None, :] * log_gamma[:, None])
    chunk_decay = jnp.exp(CHUNK * log_gamma)
    qd = jnp.broadcast_to(q_decay[:, :, None], (heads, CHUNK, RET_V_DIM))
    kd = jnp.broadcast_to(k_decay[:, :, None], (heads, CHUNK, RET_QK_DIM))
    cd = jnp.broadcast_to(chunk_decay[:, None, None], (heads, RET_QK_DIM, RET_V_DIM))
    return cos_t, sin_t, intra, qd, kd, cd


def _retention(proj, cols, gn_w, *, batch, t):
    n = proj.shape[0]
    d_ret = gn_w.shape[0]
    heads = d_ret // RET_V_DIM
    d_qk = heads * RET_QK_DIM
    nc = t // CHUNK
    cos_t, sin_t, intra, qd, kd, cd = _retention_tables(t, heads)
    row = lambda b, c: b * nc + c
    const3 = lambda b, c: (0, 0, 0)
    return pl.pallas_call(
        functools.partial(_retention_kernel, heads=heads),
        grid=(batch, nc),
        in_specs=[
            pl.BlockSpec((CHUNK, d_qk), lambda b, c: (row(b, c), cols["r_q"] // d_qk)),
            pl.BlockSpec((CHUNK, d_qk), lambda b, c: (row(b, c), cols["r_k"] // d_qk)),
            pl.BlockSpec((CHUNK, d_ret), lambda b, c: (row(b, c), cols["r_v"] // d_ret)),
            pl.BlockSpec((CHUNK, d_ret), lambda b, c: (row(b, c), cols["r_g"] // d_ret)),
            pl.BlockSpec((CHUNK, d_qk), lambda b, c: (c, 0)),
            pl.BlockSpec((CHUNK, d_qk), lambda b, c: (c, 0)),
            pl.BlockSpec(intra.shape, const3),
            pl.BlockSpec(qd.shape, const3),
            pl.BlockSpec(kd.shape, const3),
            pl.BlockSpec(cd.shape, const3),
            pl.BlockSpec((1, d_ret), lambda b, c: (0, 0)),
        ],
        out_specs=pl.BlockSpec((CHUNK, d_ret), lambda b, c: (row(b, c), 0)),
        out_shape=jax.ShapeDtypeStruct((n, d_ret), BF16),
        scratch_shapes=[pltpu.VMEM((heads, RET_QK_DIM, RET_V_DIM), F32)],
        compiler_params=_params(2),
        name="retention",
    )(proj, proj, proj, proj, cos_t, sin_t, intra, qd, kd, cd, gn_w.reshape(1, d_ret))


def _mlstm_kernel(mq_ref, mk_ref, v_ref, og_ref, gates_ref, cw_ref, cb_ref, gb_ref, gnw_ref, o_ref,
                  xprev, c_st, n_st, m_st, *, heads):
    L = CHUNK
    d = MLSTM_DIM

    @pl.when(pl.program_id(1) == 0)
    def _():
        xprev[...] = jnp.zeros_like(xprev)
        c_st[...] = jnp.zeros_like(c_st)
        n_st[...] = jnp.zeros_like(n_st)
        m_st[...] = jnp.zeros_like(m_st)

    x = jnp.concatenate([mq_ref[...], mk_ref[...]], axis=1).astype(F32)
    y = _causal_conv(x, xprev[...], cw_ref[...], cb_ref[...])
    xprev[...] = x[L - SUBLANES:L, :]
    a = y * jax.nn.sigmoid(y)
    d_m = heads * d
    q = a[:, :d_m]
    k = a[:, d_m:] * (d ** -0.5)

    gt = gates_ref[...] + gb_ref[...]
    lane = lax.broadcasted_iota(jnp.int32, gt.shape, 1)
    gate_tile = jnp.where(lane < heads, gt, jnp.where(lane < 2 * heads, jax.nn.log_sigmoid(gt), 0.0))
    r = lax.broadcasted_iota(jnp.int32, (L, L), 0)
    cidx = lax.broadcasted_iota(jnp.int32, (L, L), 1)
    causal = r >= cidx
    cum_col = jnp.dot(causal.astype(F32), gate_tile, precision=lax.Precision.HIGHEST,
                      preferred_element_type=F32)
    gate_rows = gate_tile.T
    cum_row = jnp.dot(gate_rows, (r <= cidx).astype(F32), precision=lax.Precision.HIGHEST,
                      preferred_element_type=F32)

    n_all = n_st[...]
    m_all = m_st[...]
    H = range(heads)
    hs = [slice(h * d, (h + 1) * d) for h in H]
    cum_c = [cum_col[:, heads + h:heads + h + 1] for h in H]
    cum_r = [cum_row[heads + h:heads + h + 1, :] for h in H]
    ii_r = [gate_rows[h:h + 1, :] for h in H]
    ii_c = [gate_tile[:, h:h + 1] for h in H]
    m_prev = [m_all[h:h + 1, 0:1] for h in H]
    n_prev = [n_all[h:h + 1, :] for h in H]
    c_prev = [c_st[h] for h in H]
    qh = [q[:, hs[h]] for h in H]
    kh = [k[:, hs[h]] for h in H]
    qb = [qh[h].astype(BF16) for h in H]
    vb = [v_ref[:, hs[h]].astype(BF16) for h in H]
    logw = [jnp.where(causal, cum_c[h] - cum_r[h] + ii_r[h], -jnp.inf) for h in H]
    inter = [cum_c[h] + m_prev[h] for h in H]
    m_t = [jnp.maximum(inter[h], jnp.max(logw[h], axis=-1, keepdims=True)) for h in H]
    w = [jnp.exp(logw[h] - m_t[h]) for h in H]
    sc = [jnp.exp(inter[h] - m_t[h]) for h in H]
    qk = [_dot_nt(qb[h], kh[h].astype(BF16)) * w[h] for h in H]
    num = [_dot(qk[h].astype(BF16), vb[h]) + sc[h] * _dot(qb[h], c_prev[h].astype(BF16)) for h in H]
    den = [jnp.sum(qk[h], axis=-1, keepdims=True)
           + sc[h] * jnp.sum(qh[h] * n_prev[h], axis=-1, keepdims=True) for h in H]
    hv = [num[h] / jnp.maximum(jnp.abs(den[h]), jnp.exp(-m_t[h])) for h in H]
    last = [cum_c[h][L - 1:L, :] for h in H]
    logw_end = [last[h] - cum_c[h] + ii_c[h] for h in H]
    m_new = [jnp.maximum(last[h] + m_prev[h], jnp.max(logw_end[h], axis=0, keepdims=True)) for h in H]
    kw = [kh[h] * jnp.exp(logw_end[h] - m_new[h]) for h in H]
    decay = [jnp.exp(last[h] + m_prev[h] - m_new[h]) for h in H]
    for h in H:
        c_st[h] = decay[h] * c_prev[h] + _dot_tn(kw[h].astype(BF16), vb[h])
    n_st[0:heads, :] = jnp.concatenate(
        [decay[h] * n_prev[h] + jnp.sum(kw[h], axis=0, keepdims=True) for h in H], axis=0)
    m_st[0:heads, :] = jnp.concatenate(
        [jnp.broadcast_to(m_new[h], (1, m_st.shape[1])) for h in H], axis=0)
    yn = _group_norm_heads(hv, [gnw_ref[:, hs[h]] for h in H])
    for h in H:
        o_ref[:, hs[h]] = (jax.nn.sigmoid(og_ref[:, hs[h]].astype(F32)) * yn[h]).astype(o_ref.dtype)


def _mlstm(proj, tail, cols, conv_w, conv_b, gate_b, gn_w, *, batch, t):
    n = proj.shape[0]
    d_m = gn_w.shape[0]
    heads = d_m // MLSTM_DIM
    nc = t // CHUNK
    taps = conv_w.shape[0]
    gb = jnp.zeros((1, LANES), F32).at[0, :2 * heads].set(gate_b)
    row = lambda b, c: b * nc + c
    const2 = lambda b, c: (0, 0)
    return pl.pallas_call(
        functools.partial(_mlstm_kernel, heads=heads),
        grid=(batch, nc),
        in_specs=[
            pl.BlockSpec((CHUNK, d_m), lambda b, c: (row(b, c), cols["m_q"] // d_m)),
            pl.BlockSpec((CHUNK, d_m), lambda b, c: (row(b, c), cols["m_k"] // d_m)),
            pl.BlockSpec((CHUNK, d_m), lambda b, c: (row(b, c), cols["m_v"] // d_m)),
            pl.BlockSpec((CHUNK, d_m), lambda b, c: (row(b, c), cols["m_o"] // d_m)),
            pl.BlockSpec((CHUNK, LANES), lambda b, c: (row(b, c), cols["t_gates"] // LANES)),
            pl.BlockSpec((taps, 2 * d_m), const2),
            pl.BlockSpec((1, 2 * d_m), const2),
            pl.BlockSpec((1, LANES), const2),
            pl.BlockSpec((1, d_m), const2),
        ],
        out_specs=pl.BlockSpec((CHUNK, d_m), lambda b, c: (row(b, c), 0)),
        out_shape=jax.ShapeDtypeStruct((n, d_m), BF16),
        scratch_shapes=[
            pltpu.VMEM((SUBLANES, 2 * d_m), F32),
            pltpu.VMEM((heads, MLSTM_DIM, MLSTM_DIM), F32),
            pltpu.VMEM((SUBLANES, MLSTM_DIM), F32),
            pltpu.VMEM((SUBLANES, LANES), F32),
        ],
        compiler_params=_params(2),
        name="mlstm",
    )(proj, proj, proj, proj, tail, conv_w, conv_b.reshape(1, -1), gb, gn_w.reshape(1, d_m))


def _s5_kernel(u_ref, bd_ref, w_ref, v_ref, a_ref, y_ref, t_scr, e_ref, s_ref, *, seqs, nsub):
    s = S5_SUB

    @pl.when(pl.program_id(1) == 0)
    def _():
        zero = jnp.zeros((LANES, LANES), t_scr.dtype)
        for ti in range(s):
            for to in range(s):
                t_scr[ti * LANES:(ti + 1) * LANES, to * LANES:(to + 1) * LANES] = (
                    bd_ref[to - ti] if to >= ti else zero)

    rows = seqs * nsub
    u = jnp.concatenate([u_ref[pl.ds(ti, rows, stride=s), :] for ti in range(s)], axis=1).astype(BF16)
    e_ref[...] = _dot(u, w_ref[...])
    ns = a_ref.shape[1] // 2
    ar = a_ref[:, :ns]
    ai = a_ref[:, ns:]

    def body(c, carry):
        nxt = []
        for b in range(seqs):
            re, im = carry[b]
            row = pl.ds(b * nsub + c, 1)
            s_ref[row, :ns] = re
            s_ref[row, ns:] = im
            e = e_ref[row, :]
            nxt.append((ar * re - ai * im + e[:, :ns], ar * im + ai * re + e[:, ns:]))
        return tuple(nxt)

    zero = jnp.zeros((1, ns), F32)
    lax.fori_loop(0, nsub, body, tuple((zero, zero) for _ in range(seqs)), unroll=8)
    y = _dot(u, t_scr[...]) + _dot(s_ref[...].astype(BF16), v_ref[...])
    for to in range(s):
        y_ref[pl.ds(to, rows, stride=s), :] = y[:, to * LANES:(to + 1) * LANES]


def _s5_weights(a_re, a_im, log_step, b_re, b_im, c_re, c_im):
    hp = lax.Precision.HIGHEST
    g, p = a_re.shape
    ch = b_re.shape[-1]
    s = S5_SUB
    ngl = LANES // ch
    nq = g // ngl
    ar, ai = a_re.astype(F32), a_im.astype(F32)
    step = jnp.exp(log_step.astype(F32))[:, None]
    mag = jnp.exp(ar * step)
    abar_re, abar_im = mag * jnp.cos(ai * step), mag * jnp.sin(ai * step)
    xr, xi = abar_re - 1.0, abar_im
    den = ar * ar + ai * ai
    fr, fi = (xr * ar + xi * ai) / den, (xi * ar - xr * ai) / den
    br, bi = b_re.astype(F32), b_im.astype(F32)
    bbar_re = fr[..., None] * br - fi[..., None] * bi
    bbar_im = fr[..., None] * bi + fi[..., None] * br
    j = jnp.arange(s + 1, dtype=F32)[:, None, None]
    pmag = jnp.exp(ar[None] * step[None] * j)
    pr, pi = pmag * jnp.cos(ai[None] * step[None] * j), pmag * jnp.sin(ai[None] * step[None] * j)
    cr, ci = c_re.astype(F32), c_im.astype(F32)
    cp_re = cr[None] * pr[:, :, None, :] - ci[None] * pi[:, :, None, :]
    cp_im = cr[None] * pi[:, :, None, :] + ci[None] * pr[:, :, None, :]
    kern = (jnp.einsum('jgcp,gpd->jgcd', cp_re, bbar_re, precision=hp)
            - jnp.einsum('jgcp,gpd->jgcd', cp_im, bbar_im, precision=hp))
    eye = jnp.eye(ngl, dtype=F32)
    bd = kern.reshape(s + 1, nq, ngl, ch, ch).transpose(1, 0, 2, 4, 3)
    bd = bd[:, :, :, :, None, :] * eye[None, None, :, None, :, None]
    bd = bd.reshape(nq, s + 1, LANES, LANES)
    tt = jnp.arange(s)
    prr, pir = pr[s - 1 - tt], pi[s - 1 - tt]
    w_re = prr[..., None] * bbar_re[None] - pir[..., None] * bbar_im[None]
    w_im = prr[..., None] * bbar_im[None] + pir[..., None] * bbar_re[None]

    def w_expand(w):
        w = w.reshape(s, nq, ngl, p, ch).transpose(1, 0, 2, 4, 3)
        return w[:, :, :, :, None, :] * eye[None, None, :, None, :, None]

    w_blk = jnp.stack([w_expand(w_re), w_expand(w_im)], axis=4)
    w_blk = w_blk.reshape(nq, s * LANES, 2 * ngl * p)

    def v_expand(v):
        v = v.reshape(s, nq, ngl, ch, p).transpose(1, 2, 4, 0, 3)
        return v[:, :, :, :, None, :] * eye[None, :, None, None, :, None]

    v_blk = jnp.stack([v_expand(cp_re[1:]), v_expand(-cp_im[1:])], axis=1)
    v_blk = v_blk.reshape(nq, 2 * ngl * p, s * LANES)
    a_blk = jnp.concatenate([pr[s].reshape(nq, 1, ngl * p), pi[s].reshape(nq, 1, ngl * p)], axis=-1)
    return bd.astype(BF16), w_blk.astype(BF16), v_blk.astype(BF16), a_blk


def _s5(tail, cols, a_re, a_im, log_step, b_re, b_im, c_re, c_im, *, batch, t, seqs):
    n = tail.shape[0]
    s = S5_SUB
    nsub = t // s
    bd, w_blk, v_blk, a_blk = _s5_weights(a_re, a_im, log_step, b_re, b_im, c_re, c_im)
    nq = bd.shape[0]
    kw = s * LANES
    ns2 = w_blk.shape[-1]
    rows = seqs * nsub
    c0 = cols["t_su"] // LANES
    once = dict(pipeline_mode=pl.Buffered(1))
    return pl.pallas_call(
        functools.partial(_s5_kernel, seqs=seqs, nsub=nsub),
        grid=(nq, batch // seqs),
        in_specs=[
            pl.BlockSpec((seqs * t, LANES), lambda q, i: (i, c0 + q)),
            pl.BlockSpec((None, s + 1, LANES, LANES), lambda q, i: (q, 0, 0, 0)),
            pl.BlockSpec((None, kw, ns2), lambda q, i: (q, 0, 0), **once),
            pl.BlockSpec((None, ns2, kw), lambda q, i: (q, 0, 0), **once),
            pl.BlockSpec((None, 1, ns2), lambda q, i: (q, 0, 0)),
        ],
        out_specs=pl.BlockSpec((seqs * t, LANES), lambda q, i: (i, q)),
        out_shape=jax.ShapeDtypeStruct((n, nq * LANES), F32),
        scratch_shapes=[pltpu.VMEM((kw, kw), BF16), pltpu.VMEM((rows, ns2), F32),
                        pltpu.VMEM((rows, ns2), F32)],
        compiler_params=_params(2),
        name="s5_scan",
    )(tail, bd, w_blk, v_blk, a_blk)


def _s5_glu_kernel(y_ref, u_ref, d_ref, w_ref, b_ref, o_ref):
    y = y_ref[...]
    y = y + d_ref[...] * u_ref[...]
    g = jax.nn.gelu(y)
    z = _dot(g.astype(BF16), w_ref[...]) + b_ref[...]
    o_ref[...] = (g * jax.nn.sigmoid(z)).astype(o_ref.dtype)


def _s5_glu(y, tail, cols, d_skip, glu_w_all, glu_b, layer, *, tm):
    n, d_s5 = y.shape
    const2 = lambda i: (0, 0)
    return pl.pallas_call(
        _s5_glu_kernel,
        grid=(n // tm,),
        in_specs=[
            pl.BlockSpec((tm, d_s5), lambda i: (i, 0)),
            pl.BlockSpec((tm, d_s5), lambda i: (i, cols["t_su"] // d_s5)),
            pl.BlockSpec((1, d_s5), const2),
            pl.BlockSpec((None, d_s5, d_s5), lambda i: (layer, 0, 0)),
            pl.BlockSpec((1, d_s5), const2),
        ],
        out_specs=pl.BlockSpec((tm, d_s5), lambda i: (i, 0)),
        out_shape=jax.ShapeDtypeStruct((n, d_s5), BF16),
        compiler_params=_params(1),
        name="s5_glu",
    )(y, tail, d_skip.reshape(1, d_s5), glu_w_all, glu_b.reshape(1, d_s5))


def _out_proj_kernel(x_ref, a_ref, b_ref, c_ref, wa_ref, wb_ref, wc_ref, o_ref):
    acc = _dot(a_ref[...], wa_ref[...])
    acc = acc + _dot(b_ref[...], wb_ref[...])
    acc = acc + _dot(c_ref[...], wc_ref[...])
    o_ref[...] = x_ref[...] + acc


def _out_proj(x, ret, ml, s5o, w_out_all, layer, *, tm):
    n, d = x.shape
    da, db, dc = ret.shape[1], ml.shape[1], s5o.shape[1]
    assert da == db and (da + db) % dc == 0
    return pl.pallas_call(
        _out_proj_kernel,
        grid=(n // tm,),
        in_specs=[
            pl.BlockSpec((tm, d), lambda i: (i, 0)),
            pl.BlockSpec((tm, da), lambda i: (i, 0)),
            pl.BlockSpec((tm, db), lambda i: (i, 0)),
            pl.BlockSpec((tm, dc), lambda i: (i, 0)),
            pl.BlockSpec((None, da, d), lambda i: (layer, 0, 0)),
            pl.BlockSpec((None, db, d), lambda i: (layer, 1, 0)),
            pl.BlockSpec((None, dc, d), lambda i: (layer, (da + db) // dc, 0)),
        ],
        out_specs=pl.BlockSpec((tm, d), lambda i: (i, 0)),
        out_shape=jax.ShapeDtypeStruct((n, d), F32),
        compiler_params=_params(1),
        name="out_proj",
    )(x, ret, ml, s5o, w_out_all, w_out_all, w_out_all)


FFN_CHUNKS = 2


def _ffn_kernel(x_ref, nw_ref, wv_ref, wg_ref, cw_ref, cb_ref, wd_ref, fnw_ref,
                o_ref, h_ref, acc_ref, raw0, raw1, act0, act1, halo, *, final_norm, nj):
    ti = pl.program_id(1)
    j = pl.program_id(2)
    tm = x_ref.shape[0]
    tf = wd_ref.shape[0]
    wc = tf // FFN_CHUNKS

    @pl.when(j == 0)
    def _():
        h_ref[...] = _rms(x_ref[...], nw_ref[...]).astype(h_ref.dtype)
        acc_ref[...] = jnp.zeros_like(acc_ref)

    @pl.when(jnp.logical_and(j == 0, ti == 0))
    def _():
        halo[...] = jnp.zeros_like(halo)

    def up_chunk(raw_ref, c):
        h = h_ref[...]
        cs = slice(c * wc, (c + 1) * wc)
        raw_ref[:, :wc] = _dot(h, wv_ref[:, cs])
        raw_ref[:, wc:] = _dot(h, wg_ref[:, cs])

    def gate_chunk(raw_ref, act_ref, jj, c):
        def conv(up, k):
            prev = halo[k]
            halo[k] = up[tm - SUBLANES:tm, :]
            return _causal_conv(up, prev, cw_ref[k], cb_ref[k])

        val = conv(raw_ref[:, :wc], jj * FFN_CHUNKS + c)
        gate = conv(raw_ref[:, wc:], (nj + jj) * FFN_CHUNKS + c)
        act_ref[:, c * wc:(c + 1) * wc] = (gate * jax.nn.sigmoid(gate) * val).astype(act_ref.dtype)

    def down_phase(act_ref):
        acc_ref[...] += _dot(act_ref[...], wd_ref[...])

    acts = (act0, act1)

    @pl.when(j == 0)
    def _():
        up_chunk(raw0, 0)
        up_chunk(raw1, 1)
        gate_chunk(raw0, acts[0], j, 0)

    for parity in (0, 1):
        @pl.when(jnp.logical_and(j % 2 == parity, jnp.logical_and(j >= 1, j < nj)))
        def _():
            gate_chunk(raw1, acts[1 - parity], j - 1, 1)
            up_chunk(raw0, 0)
            up_chunk(raw1, 1)
            down_phase(acts[1 - parity])
            gate_chunk(raw0, acts[parity], j, 0)

    @pl.when(j == nj)
    def _():
        gate_chunk(raw1, acts[1 - nj % 2], j - 1, 1)
        down_phase(acts[1 - nj % 2])
        y = x_ref[...] + acc_ref[...]
        if final_norm:
            y = _rms(y, fnw_ref[...])
        o_ref[...] = y


def _ffn(x, norm_w, w_up_all, conv_w, conv_b, w_down_all, final_w, layer, *, batch, t, tm, tf,
         final_norm):
    n, d = x.shape
    d_ff = w_down_all.shape[1]
    nj = d_ff // tf
    nt = t // tm
    taps = conv_w.shape[0]
    row = lambda b, i, j: (b * nt + i, 0)
    const2 = lambda b, i, j: (0, 0)
    up_j = lambda j: jnp.minimum(j, nj - 1)
    dn_j = lambda j: jnp.maximum(j - 1, 0)
    nk = 2 * nj * FFN_CHUNKS
    wc = tf // FFN_CHUNKS
    cw = conv_w.reshape(taps, nk, wc).transpose(1, 0, 2)
    cb = conv_b.reshape(nk, 1, wc)
    const3 = lambda b, i, j: (0, 0, 0)
    return pl.pallas_call(
        functools.partial(_ffn_kernel, final_norm=final_norm, nj=nj),
        grid=(batch, nt, nj + 1),
        in_specs=[
            pl.BlockSpec((tm, d), row),
            pl.BlockSpec((1, d), const2),
            pl.BlockSpec((None, d, tf), lambda b, i, j: (layer, 0, up_j(j))),
            pl.BlockSpec((None, d, tf), lambda b, i, j: (layer, 0, up_j(j) + nj)),
            pl.BlockSpec((nk, taps, wc), const3),
            pl.BlockSpec((nk, 1, wc), const3),
            pl.BlockSpec((None, tf, d), lambda b, i, j: (layer, dn_j(j), 0)),
            pl.BlockSpec((1, d), const2),
        ],
        out_specs=pl.BlockSpec((tm, d), row),
        out_shape=jax.ShapeDtypeStruct((n, d), F32),
        scratch_shapes=[
            pltpu.VMEM((tm, d), BF16),
            pltpu.VMEM((tm, d), F32),
            pltpu.VMEM((tm, 2 * wc), F32),
            pltpu.VMEM((tm, 2 * wc), F32),
            pltpu.VMEM((tm, tf), BF16),
            pltpu.VMEM((tm, tf), BF16),
            pltpu.VMEM((nk, SUBLANES, wc), F32),
        ],
        compiler_params=_params(3),
        name="conv_ffn",
    )(x, norm_w.reshape(1, d), w_up_all, w_up_all, cw, cb, w_down_all, final_w.reshape(1, d))


def _pick(n, pref):
    return pref if n % pref == 0 else n


def kernel(x, norm1_w, w_in, mlstm_conv_w, mlstm_conv_b, mlstm_gate_b, ret_gn_w, mlstm_gn_w,
           s5_A_re, s5_A_im, s5_log_step, s5_B_re, s5_B_im, s5_C_re, s5_C_im, s5_D,
           s5_glu_w, s5_glu_b, w_out, norm2_w, ffn_w_up, ffn_conv_w, ffn_conv_b, ffn_w_down,
           final_norm_w):
    batch, t, d = x.shape
    depth = w_in.shape[0]
    n = batch * t
    d_ret, d_m, d_s5 = ret_gn_w.shape[1], mlstm_gn_w.shape[1], s5_D.shape[1]
    heads_r, heads_m = d_ret // RET_V_DIM, d_m // MLSTM_DIM
    d_qk = heads_r * RET_QK_DIM
    cols, off = {}, 0
    for name, size in (("r_q", d_qk), ("r_k", d_qk), ("r_v", d_ret), ("r_g", d_ret), ("m_q", d_m),
                       ("m_k", d_m), ("m_v", d_m), ("m_o", d_m), ("m_gates", 2 * heads_m), ("s_u", d_s5)):
        cols[name] = off
        off += size
    main = cols["m_gates"]
    cols["t_su"], cols["t_gates"] = 0, d_s5
    tn = main // 6
    assert main % LANES == 0 and tn % LANES == 0 and 2 * heads_m <= LANES
    tm = _pick(t, 512)
    tf = _pick(ffn_w_down.shape[1], 512)

    w_in_b = w_in.astype(BF16)
    w_out_b = w_out.astype(BF16)
    w_up_b = ffn_w_up.astype(BF16)
    w_down_b = ffn_w_down.astype(BF16)
    glu_w_b = s5_glu_w.astype(BF16)

    xf = x.reshape(n, d)
    for l in range(depth):
        w_tail = jnp.concatenate(
            [w_in[l][:, cols["s_u"]:cols["s_u"] + d_s5], w_in[l][:, main:main + 2 * heads_m],
             jnp.zeros((d, LANES - 2 * heads_m), F32)], axis=1).astype(BF16)
        proj, tail = _in_proj(xf, norm1_w[l], w_in_b, w_tail, l, main=main, tm=_pick(t, 1024), tn=tn)
        ret = _retention(proj, cols, ret_gn_w[l], batch=batch, t=t)
        ml = _mlstm(proj, tail, cols, mlstm_conv_w[l], mlstm_conv_b[l], mlstm_gate_b[l], mlstm_gn_w[l],
                    batch=batch, t=t)
        y5 = _s5(tail, cols, s5_A_re[l], s5_A_im[l], s5_log_step[l], s5_B_re[l], s5_B_im[l],
                 s5_C_re[l], s5_C_im[l], batch=batch, t=t, seqs=2 if batch % 2 == 0 else 1)
        s5o = _s5_glu(y5, tail, cols, s5_D[l], glu_w_b, s5_glu_b[l], l, tm=tm)
        xf = _out_proj(xf, ret, ml, s5o, w_out_b, l, tm=_pick(t, 256))
        xf = _ffn(xf, norm2_w[l], w_up_b, ffn_conv_w[l], ffn_conv_b[l], w_down_b, final_norm_w, l,
                  batch=batch, t=t, tm=tm, tf=tf, final_norm=(l == depth - 1))
    return xf.reshape(batch, t, d)
```

```python
import functools

import jax
import jax.numpy as jnp
from jax import lax
from jax.experimental import pallas as pl
from jax.experimental.pallas import tpu as pltpu

RET_V_DIM = 128
RET_QK_DIM = 64
MLSTM_DIM = 128
ROPE_BASE = 10000.0
EPS = 1e-6

CHUNK = 128
S5_SUB = 16
SUBLANES = 8
LANES = 128

VMEM_LIMIT_BYTES = 56 * 1024 * 1024

F32 = jnp.float32
BF16 = jnp.bfloat16


def _params(n_grid):
    return pltpu.CompilerParams(
        dimension_semantics=("arbitrary",) * n_grid, vmem_limit_bytes=VMEM_LIMIT_BYTES)


def _rms(x, w):
    return x * lax.rsqrt(jnp.mean(x * x, axis=-1, keepdims=True) + EPS) * w


def _group_norm_heads(os, ws):
    mus = [jnp.mean(o, axis=-1, keepdims=True) for o in os]
    ds = [o - mu for o, mu in zip(os, mus)]
    vs = [jnp.mean(d * d, axis=-1, keepdims=True) for d in ds]
    return [d * lax.rsqrt(v + EPS) * w for d, v, w in zip(ds, vs, ws)]


def _dot(a, b):
    return jnp.dot(a, b, preferred_element_type=F32)


def _dot_nt(a, b):
    return lax.dot_general(a, b, (((1,), (1,)), ((), ())), preferred_element_type=F32)


def _dot_tn(a, b):
    return lax.dot_general(a, b, (((0,), (0,)), ((), ())), preferred_element_type=F32)


def _causal_conv(x, prev, cw, cb):
    taps = cw.shape[0]
    y = cb + cw[taps - 1:taps, :] * x
    row = lax.broadcasted_iota(jnp.int32, prev.shape, 0)
    for s in range(1, taps):
        sh = pltpu.roll(x, s, 0)
        head = jnp.where(row < s, pltpu.roll(prev, s, 0), sh[:SUBLANES, :])
        sh = jnp.concatenate([head, sh[SUBLANES:, :]], axis=0)
        y = y + cw[taps - 1 - s:taps - s, :] * sh
    return y


def _in_proj_kernel(x_ref, nw_ref, w_ref, wt_ref, o_ref, t_ref, h_ref):
    j = pl.program_id(1)
    last = pl.num_programs(1) - 1

    @pl.when(j == 0)
    def _():
        h_ref[...] = _rms(x_ref[...], nw_ref[...]).astype(h_ref.dtype)

    @pl.when(j < last)
    def _():
        o_ref[...] = _dot(h_ref[...], w_ref[...]).astype(o_ref.dtype)

    @pl.when(j == last)
    def _():
        t_ref[...] = _dot(h_ref[...], wt_ref[...])


def _in_proj(x, norm_w, w_all, w_tail, layer, *, tm):
    n, d = x.shape
    nj, tn = w_all.shape[1], w_all.shape[3]
    main = nj * tn
    wt = w_tail.shape[1]
    return pl.pallas_call(
        _in_proj_kernel,
        grid=(n // tm, nj + 1),
        in_specs=[
            pl.BlockSpec((tm, d), lambda i, j: (i, 0)),
            pl.BlockSpec((1, d), lambda i, j: (0, 0)),
            pl.BlockSpec((None, None, d, tn), lambda i, j: (layer, jnp.minimum(j, nj - 1), 0, 0)),
            pl.BlockSpec((d, wt), lambda i, j: (0, 0)),
        ],
        out_specs=[
            pl.BlockSpec((tm, tn), lambda i, j: (i, jnp.minimum(j, nj - 1))),
            pl.BlockSpec((tm, wt), lambda i, j: (i, 0)),
        ],
        out_shape=[
            jax.ShapeDtypeStruct((n, main), BF16),
            jax.ShapeDtypeStruct((n, wt), F32),
        ],
        scratch_shapes=[pltpu.VMEM((tm, d), BF16)],
        compiler_params=_params(2),
        name="norm_in_proj",
    )(x, norm_w.reshape(1, d), w_all, w_tail)


def _retention_kernel(q_ref, k_ref, v_ref, g_ref, cos_ref, sin_ref, intra_ref, qd_ref, kd_ref,
                      cd_ref, gnw_ref, o_ref, st_ref, *, heads):
    @pl.when(pl.program_id(1) == 0)
    def _():
        st_ref[...] = jnp.zeros_like(st_ref)

    cos = cos_ref[...]
    sin = sin_ref[...]
    width = cos.shape[1]
    half = RET_QK_DIM // 2
    first_half = (lax.broadcasted_iota(jnp.int32, cos.shape, 1) % RET_QK_DIM) < half

    def rope(x):
        swapped = jnp.where(first_half, pltpu.roll(x, width - half, 1), pltpu.roll(x, half, 1))
        return x * cos + swapped * sin

    q = rope(q_ref[...].astype(F32)) * (RET_QK_DIM ** -0.5)
    k = rope(k_ref[...].astype(F32))
    H = range(heads)
    qs = [slice(h * RET_QK_DIM, (h + 1) * RET_QK_DIM) for h in H]
    vs = [slice(h * RET_V_DIM, (h + 1) * RET_V_DIM) for h in H]
    qh = [q[:, qs[h]].astype(BF16) for h in H]
    kh = [k[:, qs[h]] for h in H]
    vh = [v_ref[:, vs[h]].astype(BF16) for h in H]
    st = [st_ref[h] for h in H]
    s = [_dot_nt(qh[h], kh[h].astype(BF16)) * intra_ref[h] for h in H]
    cross = [_dot(qh[h], st[h].astype(BF16)) * qd_ref[h] for h in H]
    inner = [_dot(s[h].astype(BF16), vh[h]) for h in H]
    for h in H:
        st_ref[h] = st[h] * cd_ref[h] + _dot_tn((kh[h] * kd_ref[h]).astype(BF16), vh[h])
    y = _group_norm_heads([inner[h] + cross[h] for h in H], [gnw_ref[:, vs[h]] for h in H])
    for h in H:
        g = g_ref[:, vs[h]].astype(F32)
        o_ref[:, vs[h]] = (g * jax.nn.sigmoid(g) * y[h]).astype(o_ref.dtype)


def _retention_tables(t, heads):
    dh = RET_QK_DIM
    inv = ROPE_BASE ** (-jnp.arange(0, dh, 2, dtype=F32) / dh)
    ang = jnp.arange(t, dtype=F32)[:, None] * inv[None, :]
    cos, sin = jnp.cos(ang), jnp.sin(ang)
    cos_t = jnp.tile(jnp.concatenate([cos, cos], axis=-1), (1, heads))
    sin_t = jnp.tile(jnp.concatenate([-sin, sin], axis=-1), (1, heads))
    log_gamma = jnp.log1p(-jnp.exp2(-5.0 - jnp.arange(heads, dtype=F32)))
    idx = jnp.arange(CHUNK, dtype=F32)
    diff = idx[:, None] - idx[None, :]
    intra = jnp.where(diff[None] >= 0,
                      jnp.exp(jnp.maximum(diff, 0.0)[None] * log_gamma[:, None, None]), 0.0)
    q_decay = jnp.exp((idx[None, :] + 1.0) * log_gamma[:, None])
    k_decay = jnp.exp((CHUNK - 1.0 - idx)[None, :] * log_gamma[:, None])
    chunk_decay = jnp.exp(CHUNK * log_gamma)
    qd = jnp.broadcast_to(q_decay[:, :, None], (heads, CHUNK, RET_V_DIM))
    kd = jnp.broadcast_to(k_decay[:, :, None], (heads, CHUNK, RET_QK_DIM))
    cd = jnp.broadcast_to(chunk_decay[:, None, None], (heads, RET_QK_DIM, RET_V_DIM))
    return cos_t, sin_t, intra, qd, kd, cd


def _retention(proj, cols, gn_w, *, batch, t):
    n = proj.shape[0]
    d_ret = gn_w.shape[0]
    heads = d_ret // RET_V_DIM
    d_qk = heads * RET_QK_DIM
    nc = t // CHUNK
    cos_t, sin_t, intra, qd, kd, cd = _retention_tables(t, heads)
    row = lambda b, c: b * nc + c
    const3 = lambda b, c: (0, 0, 0)
    return pl.pallas_call(
        functools.partial(_retention_kernel, heads=heads),
        grid=(batch, nc),
        in_specs=[
            pl.BlockSpec((CHUNK, d_qk), lambda b, c: (row(b, c), cols["r_q"] // d_qk)),
            pl.BlockSpec((CHUNK, d_qk), lambda b, c: (row(b, c), cols["r_k"] // d_qk)),
            pl.BlockSpec((CHUNK, d_ret), lambda b, c: (row(b, c), cols["r_v"] // d_ret)),
            pl.BlockSpec((CHUNK, d_ret), lambda b, c: (row(b, c), cols["r_g"] // d_ret)),
            pl.BlockSpec((CHUNK, d_qk), lambda b, c: (c, 0)),
            pl.BlockSpec((CHUNK, d_qk), lambda b, c: (c, 0)),
            pl.BlockSpec(intra.shape, const3),
            pl.BlockSpec(qd.shape, const3),
            pl.BlockSpec(kd.shape, const3),
            pl.BlockSpec(cd.shape, const3),
            pl.BlockSpec((1, d_ret), lambda b, c: (0, 0)),
        ],
        out_specs=pl.BlockSpec((CHUNK, d_ret), lambda b, c: (row(b, c), 0)),
        out_shape=jax.ShapeDtypeStruct((n, d_ret), BF16),
        scratch_shapes=[pltpu.VMEM((heads, RET_QK_DIM, RET_V_DIM), F32)],
        compiler_params=_params(2),
        name="retention",
    )(proj, proj, proj, proj, cos_t, sin_t, intra, qd, kd, cd, gn_w.reshape(1, d_ret))


def _mlstm_kernel(mq_ref, mk_ref, v_ref, og_ref, gates_ref, cw_ref, cb_ref, gb_ref, gnw_ref, o_ref,
                  xprev, c_st, n_st, m_st, *, heads):
    L = CHUNK
    d = MLSTM_DIM

    @pl.when(pl.program_id(1) == 0)
    def _():
        xprev[...] = jnp.zeros_like(xprev)
        c_st[...] = jnp.zeros_like(c_st)
        n_st[...] = jnp.zeros_like(n_st)
        m_st[...] = jnp.zeros_like(m_st)

    x = jnp.concatenate([mq_ref[...], mk_ref[...]], axis=1).astype(F32)
    y = _causal_conv(x, xprev[...], cw_ref[...], cb_ref[...])
    xprev[...] = x[L - SUBLANES:L, :]
    a = y * jax.nn.sigmoid(y)
    d_m = heads * d
    q = a[:, :d_m]
    k = a[:, d_m:] * (d ** -0.5)

    gt = gates_ref[...] + gb_ref[...]
    lane = lax.broadcasted_iota(jnp.int32, gt.shape, 1)
    gate_tile = jnp.where(lane < heads, gt, jnp.where(lane < 2 * heads, jax.nn.log_sigmoid(gt), 0.0))
    r = lax.broadcasted_iota(jnp.int32, (L, L), 0)
    cidx = lax.broadcasted_iota(jnp.int32, (L, L), 1)
    causal = r >= cidx
    cum_col = jnp.dot(causal.astype(F32), gate_tile, precision=lax.Precision.HIGHEST,
                      preferred_element_type=F32)
    gate_rows = gate_tile.T
    cum_row = jnp.dot(gate_rows, (r <= cidx).astype(F32), precision=lax.Precision.HIGHEST,
                      preferred_element_type=F32)

    n_all = n_st[...]
    m_all = m_st[...]
    H = range(heads)
    hs = [slice(h * d, (h + 1) * d) for h in H]
    cum_c = [cum_col[:, heads + h:heads + h + 1] for h in H]
    cum_r = [cum_row[heads + h:heads + h + 1, :] for h in H]
    ii_r = [gate_rows[h:h + 1, :] for h in H]
    ii_c = [gate_tile[:, h:h + 1] for h in H]
    m_prev = [m_all[h:h + 1, 0:1] for h in H]
    n_prev = [n_all[h:h + 1, :] for h in H]
    c_prev = [c_st[h] for h in H]
    qh = [q[:, hs[h]] for h in H]
    kh = [k[:, hs[h]] for h in H]
    qb = [qh[h].astype(BF16) for h in H]
    vb = [v_ref[:, hs[h]].astype(BF16) for h in H]
    logw = [jnp.where(causal, cum_c[h] - cum_r[h] + ii_r[h], -jnp.inf) for h in H]
    inter = [cum_c[h] + m_prev[h] for h in H]
    m_t = [jnp.maximum(inter[h], jnp.max(logw[h], axis=-1, keepdims=True)) for h in H]
    w = [jnp.exp(logw[h] - m_t[h]) for h in H]
    sc = [jnp.exp(inter[h] - m_t[h]) for h in H]
    qk = [_dot_nt(qb[h], kh[h].astype(BF16)) * w[h] for h in H]
    num = [_dot(qk[h].astype(BF16), vb[h]) + sc[h] * _dot(qb[h], c_prev[h].astype(BF16)) for h in H]
    den = [jnp.sum(qk[h], axis=-1, keepdims=True)
           + sc[h] * jnp.sum(qh[h] * n_prev[h], axis=-1, keepdims=True) for h in H]
    hv = [num[h] / jnp.maximum(jnp.abs(den[h]), jnp.exp(-m_t[h])) for h in H]
    last = [cum_c[h][L - 1:L, :] for h in H]
    logw_end = [last[h] - cum_c[h] + ii_c[h] for h in H]
    m_new = [jnp.maximum(last[h] + m_prev[h], jnp.max(logw_end[h], axis=0, keepdims=True)) for h in H]
    kw = [kh[h] * jnp.exp(logw_end[h] - m_new[h]) for h in H]
    decay = [jnp.exp(last[h] + m_prev[h] - m_new[h]) for h in H]
    for h in H:
        c_st[h] = decay[h] * c_prev[h] + _dot_tn(kw[h].astype(BF16), vb[h])
    n_st[0:heads, :] = jnp.concatenate(
        [decay[h] * n_prev[h] + jnp.sum(kw[h], axis=0, keepdims=True) for h in H], axis=0)
    m_st[0:heads, :] = jnp.concatenate(
        [jnp.broadcast_to(m_new[h], (1, m_st.shape[1])) for h in H], axis=0)
    yn = _group_norm_heads(hv, [gnw_ref[:, hs[h]] for h in H])
    for h in H:
        o_ref[:, hs[h]] = (jax.nn.sigmoid(og_ref[:, hs[h]].astype(F32)) * yn[h]).astype(o_ref.dtype)


def _mlstm(proj, tail, cols, conv_w, conv_b, gate_b, gn_w, *, batch, t):
    n = proj.shape[0]
    d_m = gn_w.shape[0]
    heads = d_m // MLSTM_DIM
    nc = t // CHUNK
    taps = conv_w.shape[0]
    gb = jnp.zeros((1, LANES), F32).at[0, :2 * heads].set(gate_b)
    row = lambda b, c: b * nc + c
    const2 = lambda b, c: (0, 0)
    return pl.pallas_call(
        functools.partial(_mlstm_kernel, heads=heads),
        grid=(batch, nc),
        in_specs=[
            pl.BlockSpec((CHUNK, d_m), lambda b, c: (row(b, c), cols["m_q"] // d_m)),
            pl.BlockSpec((CHUNK, d_m), lambda b, c: (row(b, c), cols["m_k"] // d_m)),
            pl.BlockSpec((CHUNK, d_m), lambda b, c: (row(b, c), cols["m_v"] // d_m)),
            pl.BlockSpec((CHUNK, d_m), lambda b, c: (row(b, c), cols["m_o"] // d_m)),
            pl.BlockSpec((CHUNK, LANES), lambda b, c: (row(b, c), cols["t_gates"] // LANES)),
            pl.BlockSpec((taps, 2 * d_m), const2),
            pl.BlockSpec((1, 2 * d_m), const2),
            pl.BlockSpec((1, LANES), const2),
            pl.BlockSpec((1, d_m), const2),
        ],
        out_specs=pl.BlockSpec((CHUNK, d_m), lambda b, c: (row(b, c), 0)),
        out_shape=jax.ShapeDtypeStruct((n, d_m), BF16),
        scratch_shapes=[
            pltpu.VMEM((SUBLANES, 2 * d_m), F32),
            pltpu.VMEM((heads, MLSTM_DIM, MLSTM_DIM), F32),
            pltpu.VMEM((SUBLANES, MLSTM_DIM), F32),
            pltpu.VMEM((SUBLANES, LANES), F32),
        ],
        compiler_params=_params(2),
        name="mlstm",
    )(proj, proj, proj, proj, tail, conv_w, conv_b.reshape(1, -1), gb, gn_w.reshape(1, d_m))


def _s5_kernel(u_ref, bd_ref, w_ref, v_ref, a_ref, y_ref, t_scr, e_ref, s_ref, *, seqs, nsub):
    s = S5_SUB

    @pl.when(pl.program_id(1) == 0)
    def _():
        zero = jnp.zeros((LANES, LANES), t_scr.dtype)
        for ti in range(s):
            for to in range(s):
                t_scr[ti * LANES:(ti + 1) * LANES, to * LANES:(to + 1) * LANES] = (
                    bd_ref[to - ti] if to >= ti else zero)

    rows = seqs * nsub
    u = jnp.concatenate([u_ref[pl.ds(ti, rows, stride=s), :] for ti in range(s)], axis=1).astype(BF16)
    e_ref[...] = _dot(u, w_ref[...])
    ns = a_ref.shape[1] // 2
    ar = a_ref[:, :ns]
    ai = a_ref[:, ns:]

    def body(c, carry):
        nxt = []
        for b in range(seqs):
            re, im = carry[b]
            row = pl.ds(b * nsub + c, 1)
            s_ref[row, :ns] = re
            s_ref[row, ns:] = im
            e = e_ref[row, :]
            nxt.append((ar * re - ai * im + e[:, :ns], ar * im + ai * re + e[:, ns:]))
        return tuple(nxt)

    zero = jnp.zeros((1, ns), F32)
    lax.fori_loop(0, nsub, body, tuple((zero, zero) for _ in range(seqs)), unroll=8)
    y = _dot(u, t_scr[...]) + _dot(s_ref[...].astype(BF16), v_ref[...])
    for to in range(s):
        y_ref[pl.ds(to, rows, stride=s), :] = y[:, to * LANES:(to + 1) * LANES]


def _s5_weights(a_re, a_im, log_step, b_re, b_im, c_re, c_im):
    hp = lax.Precision.HIGHEST
    g, p = a_re.shape
    ch = b_re.shape[-1]
    s = S5_SUB
    ngl = LANES // ch
    nq = g // ngl
    ar, ai = a_re.astype(F32), a_im.astype(F32)
    step = jnp.exp(log_step.astype(F32))[:, None]
    mag = jnp.exp(ar * step)
    abar_re, abar_im = mag * jnp.cos(ai * step), mag * jnp.sin(ai * step)
    xr, xi = abar_re - 1.0, abar_im
    den = ar * ar + ai * ai
    fr, fi = (xr * ar + xi * ai) / den, (xi * ar - xr * ai) / den
    br, bi = b_re.astype(F32), b_im.astype(F32)
    bbar_re = fr[..., None] * br - fi[..., None] * bi
    bbar_im = fr[..., None] * bi + fi[..., None] * br
    j = jnp.arange(s + 1, dtype=F32)[:, None, None]
    pmag = jnp.exp(ar[None] * step[None] * j)
    pr, pi = pmag * jnp.cos(ai[None] * step[None] * j), pmag * jnp.sin(ai[None] * step[None] * j)
    cr, ci = c_re.astype(F32), c_im.astype(F32)
    cp_re = cr[None] * pr[:, :, None, :] - ci[None] * pi[:, :, None, :]
    cp_im = cr[None] * pi[:, :, None, :] + ci[None] * pr[:, :, None, :]
    kern = (jnp.einsum('jgcp,gpd->jgcd', cp_re, bbar_re, precision=hp)
            - jnp.einsum('jgcp,gpd->jgcd', cp_im, bbar_im, precision=hp))
    eye = jnp.eye(ngl, dtype=F32)
    bd = kern.reshape(s + 1, nq, ngl, ch, ch).transpose(1, 0, 2, 4, 3)
    bd = bd[:, :, :, :, None, :] * eye[None, None, :, None, :, None]
    bd = bd.reshape(nq, s + 1, LANES, LANES)
    tt = jnp.arange(s)
    prr, pir = pr[s - 1 - tt], pi[s - 1 - tt]
    w_re = prr[..., None] * bbar_re[None] - pir[..., None] * bbar_im[None]
    w_im = prr[..., None] * bbar_im[None] + pir[..., None] * bbar_re[None]

    def w_expand(w):
        w = w.reshape(s, nq, ngl, p, ch).transpose(1, 0, 2, 4, 3)
        return w[:, :, :, :, None, :] * eye[None, None, :, None, :, None]

    w_blk = jnp.stack([w_expand(w_re), w_expand(w_im)], axis=4)
    w_blk = w_blk.reshape(nq, s * LANES, 2 * ngl * p)

    def v_expand(v):
        v = v.reshape(s, nq, ngl, ch, p).transpose(1, 2, 4, 0, 3)
        return v[:, :, :, :, None, :] * eye[None, :, None, None, :, None]

    v_blk = jnp.stack([v_expand(cp_re[1:]), v_expand(-cp_im[1:])], axis=1)
    v_blk = v_blk.reshape(nq, 2 * ngl * p, s * LANES)
    a_blk = jnp.concatenate([pr[s].reshape(nq, 1, ngl * p), pi[s].reshape(nq, 1, ngl * p)], axis=-1)
    return bd.astype(BF16), w_blk.astype(BF16), v_blk.astype(BF16), a_blk


def _s5(tail, cols, a_re, a_im, log_step, b_re, b_im, c_re, c_im, *, batch, t, seqs):
    n = tail.shape[0]
    s = S5_SUB
    nsub = t // s
    bd, w_blk, v_blk, a_blk = _s5_weights(a_re, a_im, log_step, b_re, b_im, c_re, c_im)
    nq = bd.shape[0]
    kw = s * LANES
    ns2 = w_blk.shape[-1]
    rows = seqs * nsub
    c0 = cols["t_su"] // LANES
    once = dict(pipeline_mode=pl.Buffered(1))
    return pl.pallas_call(
        functools.partial(_s5_kernel, seqs=seqs, nsub=nsub),
        grid=(nq, batch // seqs),
        in_specs=[
            pl.BlockSpec((seqs * t, LANES), lambda q, i: (i, c0 + q)),
            pl.BlockSpec((None, s + 1, LANES, LANES), lambda q, i: (q, 0, 0, 0)),
            pl.BlockSpec((None, kw, ns2), lambda q, i: (q, 0, 0), **once),
            pl.BlockSpec((None, ns2, kw), lambda q, i: (q, 0, 0), **once),
            pl.BlockSpec((None, 1, ns2), lambda q, i: (q, 0, 0)),
        ],
        out_specs=pl.BlockSpec((seqs * t, LANES), lambda q, i: (i, q)),
        out_shape=jax.ShapeDtypeStruct((n, nq * LANES), F32),
        scratch_shapes=[pltpu.VMEM((kw, kw), BF16), pltpu.VMEM((rows, ns2), F32),
                        pltpu.VMEM((rows, ns2), F32)],
        compiler_params=_params(2),
        name="s5_scan",
    )(tail, bd, w_blk, v_blk, a_blk)


def _s5_glu_kernel(y_ref, u_ref, d_ref, w_ref, b_ref, o_ref):
    y = y_ref[...]
    y = y + d_ref[...] * u_ref[...]
    g = jax.nn.gelu(y)
    z = _dot(g.astype(BF16), w_ref[...]) + b_ref[...]
    o_ref[...] = (g * jax.nn.sigmoid(z)).astype(o_ref.dtype)


def _s5_glu(y, tail, cols, d_skip, glu_w_all, glu_b, layer, *, tm):
    n, d_s5 = y.shape
    const2 = lambda i: (0, 0)
    return pl.pallas_call(
        _s5_glu_kernel,
        grid=(n // tm,),
        in_specs=[
            pl.BlockSpec((tm, d_s5), lambda i: (i, 0)),
            pl.BlockSpec((tm, d_s5), lambda i: (i, cols["t_su"] // d_s5)),
            pl.BlockSpec((1, d_s5), const2),
            pl.BlockSpec((None, d_s5, d_s5), lambda i: (layer, 0, 0)),
            pl.BlockSpec((1, d_s5), const2),
        ],
        out_specs=pl.BlockSpec((tm, d_s5), lambda i: (i, 0)),
        out_shape=jax.ShapeDtypeStruct((n, d_s5), BF16),
        compiler_params=_params(1),
        name="s5_glu",
    )(y, tail, d_skip.reshape(1, d_s5), glu_w_all, glu_b.reshape(1, d_s5))


def _out_proj_kernel(x_ref, a_ref, b_ref, c_ref, wa_ref, wb_ref, wc_ref, o_ref):
    acc = _dot(a_ref[...], wa_ref[...])
    acc = acc + _dot(b_ref[...], wb_ref[...])
    acc = acc + _dot(c_ref[...], wc_ref[...])
    o_ref[...] = x_ref[...] + acc


def _out_proj(x, ret, ml, s5o, w_out_all, layer, *, tm):
    n, d = x.shape
    da, db, dc = ret.shape[1], ml.shape[1], s5o.shape[1]
    assert da == db and (da + db) % dc == 0
    return pl.pallas_call(
        _out_proj_kernel,
        grid=(n // tm,),
        in_specs=[
            pl.BlockSpec((tm, d), lambda i: (i, 0)),
            pl.BlockSpec((tm, da), lambda i: (i, 0)),
            pl.BlockSpec((tm, db), lambda i: (i, 0)),
            pl.BlockSpec((tm, dc), lambda i: (i, 0)),
            pl.BlockSpec((None, da, d), lambda i: (layer, 0, 0)),
            pl.BlockSpec((None, db, d), lambda i: (layer, 1, 0)),
            pl.BlockSpec((None, dc, d), lambda i: (layer, (da + db) // dc, 0)),
        ],
        out_specs=pl.BlockSpec((tm, d), lambda i: (i, 0)),
        out_shape=jax.ShapeDtypeStruct((n, d), F32),
        compiler_params=_params(1),
        name="out_proj",
    )(x, ret, ml, s5o, w_out_all, w_out_all, w_out_all)


FFN_CHUNKS = 2


def _ffn_kernel(x_ref, nw_ref, wv_ref, wg_ref, cw_ref, cb_ref, wd_ref, fnw_ref,
                o_ref, h_ref, raw0, raw1, act0, act1, halo, *, final_norm, nj):
    ti = pl.program_id(1)
    j = pl.program_id(2)
    tm = x_ref.shape[0]
    tf = wd_ref.shape[0]
    wc = tf // FFN_CHUNKS

    @pl.when(j == 0)
    def _():
        x = x_ref[...]
        h_ref[...] = _rms(x, nw_ref[...]).astype(h_ref.dtype)
        o_ref[...] = x

    @pl.when(jnp.logical_and(j == 0, ti == 0))
    def _():
        halo[...] = jnp.zeros_like(halo)

    def up_chunk(raw_ref, c):
        h = h_ref[...]
        cs = slice(c * wc, (c + 1) * wc)
        raw_ref[:, :wc] = _dot(h, wv_ref[:, cs])
        raw_ref[:, wc:] = _dot(h, wg_ref[:, cs])

    def gate_chunk(raw_ref, act_ref, jj, c):
        def conv(up, k):
            prev = halo[k]
            halo[k] = up[tm - SUBLANES:tm, :]
            return _causal_conv(up, prev, cw_ref[k], cb_ref[k])

        val = conv(raw_ref[:, :wc], jj * FFN_CHUNKS + c)
        gate = conv(raw_ref[:, wc:], (nj + jj) * FFN_CHUNKS + c)
        act_ref[:, c * wc:(c + 1) * wc] = (gate * jax.nn.sigmoid(gate) * val).astype(act_ref.dtype)

    def down_phase(act_ref):
        o_ref[...] += _dot(act_ref[...], wd_ref[...])

    acts = (act0, act1)

    @pl.when(j == 0)
    def _():
        up_chunk(raw0, 0)
        up_chunk(raw1, 1)
        gate_chunk(raw0, acts[0], j, 0)

    for parity in (0, 1):
        @pl.when(jnp.logical_and(j % 2 == parity, jnp.logical_and(j >= 1, j < nj)))
        def _():
            gate_chunk(raw1, acts[1 - parity], j - 1, 1)
            up_chunk(raw0, 0)
            up_chunk(raw1, 1)
            down_phase(acts[1 - parity])
            gate_chunk(raw0, acts[parity], j, 0)

    @pl.when(j == nj)
    def _():
        gate_chunk(raw1, acts[1 - nj % 2], j - 1, 1)
        down_phase(acts[1 - nj % 2])
        if final_norm:
            o_ref[...] = _rms(o_ref[...], fnw_ref[...])


def _ffn(x, norm_w, w_up_all, conv_w, conv_b, w_down_all, final_w, layer, *, batch, t, tm, tf,
         final_norm):
    n, d = x.shape
    d_ff = w_down_all.shape[1]
    nj = d_ff // tf
    nt = t // tm
    taps = conv_w.shape[0]
    row = lambda b, i, j: (b * nt + i, 0)
    const2 = lambda b, i, j: (0, 0)
    up_j = lambda j: jnp.minimum(j, nj - 1)
    dn_j = lambda j: jnp.maximum(j - 1, 0)
    nk = 2 * nj * FFN_CHUNKS
    wc = tf // FFN_CHUNKS
    cw = conv_w.reshape(taps, nk, wc).transpose(1, 0, 2)
    cb = conv_b.reshape(nk, 1, wc)
    const3 = lambda b, i, j: (0, 0, 0)
    return pl.pallas_call(
        functools.partial(_ffn_kernel, final_norm=final_norm, nj=nj),
        grid=(batch, nt, nj + 1),
        in_specs=[
            pl.BlockSpec((tm, d), row, pipeline_mode=pl.Buffered(1)),
            pl.BlockSpec((1, d), const2),
            pl.BlockSpec((None, None, d, tf), lambda b, i, j: (layer, up_j(j), 0, 0)),
            pl.BlockSpec((None, None, d, tf), lambda b, i, j: (layer, up_j(j) + nj, 0, 0)),
            pl.BlockSpec((nk, taps, wc), const3),
            pl.BlockSpec((nk, 1, wc), const3),
            pl.BlockSpec((None, tf, d), lambda b, i, j: (layer, dn_j(j), 0)),
            pl.BlockSpec((1, d), const2),
        ],
        out_specs=pl.BlockSpec((tm, d), row),
        out_shape=jax.ShapeDtypeStruct((n, d), F32),
        scratch_shapes=[
            pltpu.VMEM((tm, d), BF16),
            pltpu.VMEM((tm, 2 * wc), F32),
            pltpu.VMEM((tm, 2 * wc), F32),
            pltpu.VMEM((tm, tf), BF16),
            pltpu.VMEM((tm, tf), BF16),
            pltpu.VMEM((nk, SUBLANES, wc), F32),
        ],
        compiler_params=_params(3),
        name="conv_ffn",
    )(x, norm_w.reshape(1, d), w_up_all, w_up_all, cw, cb, w_down_all, final_w.reshape(1, d))


def _pick(n, pref):
    return pref if n % pref == 0 else n


def kernel(x, norm1_w, w_in, mlstm_conv_w, mlstm_conv_b, mlstm_gate_b, ret_gn_w, mlstm_gn_w,
           s5_A_re, s5_A_im, s5_log_step, s5_B_re, s5_B_im, s5_C_re, s5_C_im, s5_D,
           s5_glu_w, s5_glu_b, w_out, norm2_w, ffn_w_up, ffn_conv_w, ffn_conv_b, ffn_w_down,
           final_norm_w):
    batch, t, d = x.shape
    depth = w_in.shape[0]
    n = batch * t
    d_ret, d_m, d_s5 = ret_gn_w.shape[1], mlstm_gn_w.shape[1], s5_D.shape[1]
    heads_r, heads_m = d_ret // RET_V_DIM, d_m // MLSTM_DIM
    d_qk = heads_r * RET_QK_DIM
    cols, off = {}, 0
    for name, size in (("r_q", d_qk), ("r_k", d_qk), ("r_v", d_ret), ("r_g", d_ret), ("m_q", d_m),
                       ("m_k", d_m), ("m_v", d_m), ("m_o", d_m), ("m_gates", 2 * heads_m), ("s_u", d_s5)):
        cols[name] = off
        off += size
    main = cols["m_gates"]
    cols["t_su"], cols["t_gates"] = 0, d_s5
    tn = main // 6
    assert main % LANES == 0 and tn % LANES == 0 and 2 * heads_m <= LANES
    tm = _pick(t, 512)
    tf = _pick(ffn_w_down.shape[1], 512)

    w_in_b = w_in[:, :, :main].astype(BF16).reshape(depth, d, main // tn, tn).transpose(0, 2, 1, 3)
    w_out_b = w_out.astype(BF16)
    w_up_b = ffn_w_up.astype(BF16).reshape(depth, d, -1, tf).transpose(0, 2, 1, 3)
    w_down_b = ffn_w_down.astype(BF16)
    glu_w_b = s5_glu_w.astype(BF16)

    xf = x.reshape(n, d)
    for l in range(depth):
        w_tail = jnp.concatenate(
            [w_in[l][:, cols["s_u"]:cols["s_u"] + d_s5], w_in[l][:, main:main + 2 * heads_m],
             jnp.zeros((d, LANES - 2 * heads_m), F32)], axis=1).astype(BF16)
        proj, tail = _in_proj(xf, norm1_w[l], w_in_b, w_tail, l, tm=_pick(t, 1024))
        ret = _retention(proj, cols, ret_gn_w[l], batch=batch, t=t)
        ml = _mlstm(proj, tail, cols, mlstm_conv_w[l], mlstm_conv_b[l], mlstm_gate_b[l], mlstm_gn_w[l],
                    batch=batch, t=t)
        y5 = _s5(tail, cols, s5_A_re[l], s5_A_im[l], s5_log_step[l], s5_B_re[l], s5_B_im[l],
                 s5_C_re[l], s5_C_im[l], batch=batch, t=t, seqs=2 if batch % 2 == 0 else 1)
        s5o = _s5_glu(y5, tail, cols, s5_D[l], glu_w_b, s5_glu_b[l], l, tm=tm)
        xf = _out_proj(xf, ret, ml, s5o, w_out_b, l, tm=_pick(t, 256))
        xf = _ffn(xf, norm2_w[l], w_up_b, ffn_conv_w[l], ffn_conv_b[l], w_down_b, final_norm_w, l,
                  batch=batch, t=t, tm=_pick(t, 1024), tf=tf, final_norm=(l == depth - 1))
    return xf.reshape(batch, t, d)
```

```python
import functools

import jax
import jax.numpy as jnp
from jax import lax
from jax.experimental import pallas as pl
from jax.experimental.pallas import tpu as pltpu

RET_V_DIM = 128
RET_QK_DIM = 64
MLSTM_DIM = 128
ROPE_BASE = 10000.0
EPS = 1e-6

CHUNK = 128
S5_SUB = 16
SUBLANES = 8
LANES = 128

VMEM_LIMIT_BYTES = 56 * 1024 * 1024

F32 = jnp.float32
BF16 = jnp.bfloat16


def _params(n_grid):
    return pltpu.CompilerParams(
        dimension_semantics=("arbitrary",) * n_grid, vmem_limit_bytes=VMEM_LIMIT_BYTES)


def _rms(x, w):
    return x * lax.rsqrt(jnp.mean(x * x, axis=-1, keepdims=True) + EPS) * w


def _group_norm_heads(os, ws):
    mus = [jnp.mean(o, axis=-1, keepdims=True) for o in os]
    ds = [o - mu for o, mu in zip(os, mus)]
    vs = [jnp.mean(d * d, axis=-1, keepdims=True) for d in ds]
    return [d * lax.rsqrt(v + EPS) * w for d, v, w in zip(ds, vs, ws)]


def _dot(a, b):
    return jnp.dot(a, b, preferred_element_type=F32)


def _dot_nt(a, b):
    return lax.dot_general(a, b, (((1,), (1,)), ((), ())), preferred_element_type=F32)


def _dot_tn(a, b):
    return lax.dot_general(a, b, (((0,), (0,)), ((), ())), preferred_element_type=F32)


def _causal_conv(x, prev, cw, cb):
    taps = cw.shape[0]
    y = cb + cw[taps - 1:taps, :] * x
    row = lax.broadcasted_iota(jnp.int32, prev.shape, 0)
    for s in range(1, taps):
        sh = pltpu.roll(x, s, 0)
        head = jnp.where(row < s, pltpu.roll(prev, s, 0), sh[:SUBLANES, :])
        sh = jnp.concatenate([head, sh[SUBLANES:, :]], axis=0)
        y = y + cw[taps - 1 - s:taps - s, :] * sh
    return y


def _in_proj_kernel(x_ref, nw_ref, w_ref, wt_ref, o_ref, t_ref, h_ref):
    j = pl.program_id(1)
    last = pl.num_programs(1) - 1

    @pl.when(j == 0)
    def _():
        h_ref[...] = _rms(x_ref[...], nw_ref[...]).astype(h_ref.dtype)

    @pl.when(j < last)
    def _():
        o_ref[...] = _dot(h_ref[...], w_ref[...]).astype(o_ref.dtype)

    @pl.when(j == last)
    def _():
        t_ref[...] = _dot(h_ref[...], wt_ref[...])


def _in_proj(x, norm_w, w_all, w_tail, layer, *, main, tm, tn):
    n, d = x.shape
    nj = main // tn
    wt = w_tail.shape[1]
    return pl.pallas_call(
        _in_proj_kernel,
        grid=(n // tm, nj + 1),
        in_specs=[
            pl.BlockSpec((tm, d), lambda i, j: (i, 0)),
            pl.BlockSpec((1, d), lambda i, j: (0, 0)),
            pl.BlockSpec((None, d, tn), lambda i, j: (layer, 0, jnp.minimum(j, nj - 1))),
            pl.BlockSpec((d, wt), lambda i, j: (0, 0)),
        ],
        out_specs=[
            pl.BlockSpec((tm, tn), lambda i, j: (i, jnp.minimum(j, nj - 1))),
            pl.BlockSpec((tm, wt), lambda i, j: (i, 0)),
        ],
        out_shape=[
            jax.ShapeDtypeStruct((n, main), BF16),
            jax.ShapeDtypeStruct((n, wt), F32),
        ],
        scratch_shapes=[pltpu.VMEM((tm, d), BF16)],
        compiler_params=_params(2),
        name="norm_in_proj",
    )(x, norm_w.reshape(1, d), w_all, w_tail)


def _retention_kernel(q_ref, k_ref, v_ref, g_ref, cos_ref, sin_ref, intra_ref, qd_ref, kd_ref,
                      cd_ref, gnw_ref, o_ref, st_ref, *, heads):
    @pl.when(pl.program_id(1) == 0)
    def _():
        st_ref[...] = jnp.zeros_like(st_ref)

    cos = cos_ref[...]
    sin = sin_ref[...]
    width = cos.shape[1]
    half = RET_QK_DIM // 2
    first_half = (lax.broadcasted_iota(jnp.int32, cos.shape, 1) % RET_QK_DIM) < half

    def rope(x):
        swapped = jnp.where(first_half, pltpu.roll(x, width - half, 1), pltpu.roll(x, half, 1))
        return x * cos + swapped * sin

    q = rope(q_ref[...].astype(F32)) * (RET_QK_DIM ** -0.5)
    k = rope(k_ref[...].astype(F32))
    H = range(heads)
    qs = [slice(h * RET_QK_DIM, (h + 1) * RET_QK_DIM) for h in H]
    vs = [slice(h * RET_V_DIM, (h + 1) * RET_V_DIM) for h in H]
    qh = [q[:, qs[h]].astype(BF16) for h in H]
    kh = [k[:, qs[h]] for h in H]
    vh = [v_ref[:, vs[h]].astype(BF16) for h in H]
    st = [st_ref[h] for h in H]
    s = [_dot_nt(qh[h], kh[h].astype(BF16)) * intra_ref[h] for h in H]
    cross = [_dot(qh[h], st[h].astype(BF16)) * qd_ref[h] for h in H]
    inner = [_dot(s[h].astype(BF16), vh[h]) for h in H]
    for h in H:
        st_ref[h] = st[h] * cd_ref[h] + _dot_tn((kh[h] * kd_ref[h]).astype(BF16), vh[h])
    y = _group_norm_heads([inner[h] + cross[h] for h in H], [gnw_ref[:, vs[h]] for h in H])
    for h in H:
        g = g_ref[:, vs[h]].astype(F32)
        o_ref[:, vs[h]] = (g * jax.nn.sigmoid(g) * y[h]).astype(o_ref.dtype)


def _retention_tables(t, heads):
    dh = RET_QK_DIM
    inv = ROPE_BASE ** (-jnp.arange(0, dh, 2, dtype=F32) / dh)
    ang = jnp.arange(t, dtype=F32)[:, None] * inv[None, :]
    cos, sin = jnp.cos(ang), jnp.sin(ang)
    cos_t = jnp.tile(jnp.concatenate([cos, cos], axis=-1), (1, heads))
    sin_t = jnp.tile(jnp.concatenate([-sin, sin], axis=-1), (1, heads))
    log_gamma = jnp.log1p(-jnp.exp2(-5.0 - jnp.arange(heads, dtype=F32)))
    idx = jnp.arange(CHUNK, dtype=F32)
    diff = idx[:, None] - idx[None, :]
    intra = jnp.where(diff[None] >= 0,
                      jnp.exp(jnp.maximum(diff, 0.0)[None] * log_gamma[:, None, None]), 0.0)
    q_decay = jnp.exp((idx[None, :] + 1.0) * log_gamma[:, None])
    k_decay = jnp.exp((CHUNK - 1.0 - idx)[None, :] * log_gamma[:, None])
    chunk_decay = jnp.exp(CHUNK * log_gamma)
    qd = jnp.broadcast_to(q_decay[:, :, None], (heads, CHUNK, RET_V_DIM))
    kd = jnp.broadcast_to(k_decay[:, :, None], (heads, CHUNK, RET_QK_DIM))
    cd = jnp.broadcast_to(chunk_decay[:, None, None], (heads, RET_QK_DIM, RET_V_DIM))
    return cos_t, sin_t, intra, qd, kd, cd


def _retention(proj, cols, gn_w, *, batch, t):
    n = proj.shape[0]
    d_ret = gn_w.shape[0]
    heads = d_ret // RET_V_DIM
    d_qk = heads * RET_QK_DIM
    nc = t // CHUNK
    cos_t, sin_t, intra, qd, kd, cd = _retention_tables(t, heads)
    row = lambda b, c: b * nc + c
    const3 = lambda b, c: (0, 0, 0)
    return pl.pallas_call(
        functools.partial(_retention_kernel, heads=heads),
        grid=(batch, nc),
        in_specs=[
            pl.BlockSpec((CHUNK, d_qk), lambda b, c: (row(b, c), cols["r_q"] // d_qk)),
            pl.BlockSpec((CHUNK, d_qk), lambda b, c: (row(b, c), cols["r_k"] // d_qk)),
            pl.BlockSpec((CHUNK, d_ret), lambda b, c: (row(b, c), cols["r_v"] // d_ret)),
            pl.BlockSpec((CHUNK, d_ret), lambda b, c: (row(b, c), cols["r_g"] // d_ret)),
            pl.BlockSpec((CHUNK, d_qk), lambda b, c: (c, 0)),
            pl.BlockSpec((CHUNK, d_qk), lambda b, c: (c, 0)),
            pl.BlockSpec(intra.shape, const3),
            pl.BlockSpec(qd.shape, const3),
            pl.BlockSpec(kd.shape, const3),
            pl.BlockSpec(cd.shape, const3),
            pl.BlockSpec((1, d_ret), lambda b, c: (0, 0)),
        ],
        out_specs=pl.BlockSpec((CHUNK, d_ret), lambda b, c: (row(b, c), 0)),
        out_shape=jax.ShapeDtypeStruct((n, d_ret), BF16),
        scratch_shapes=[pltpu.VMEM((heads, RET_QK_DIM, RET_V_DIM), F32)],
        compiler_params=_params(2),
        name="retention",
    )(proj, proj, proj, proj, cos_t, sin_t, intra, qd, kd, cd, gn_w.reshape(1, d_ret))


def _mlstm_kernel(mq_ref, mk_ref, v_ref, og_ref, gates_ref, cw_ref, cb_ref, gb_ref, gnw_ref, o_ref,
                  xprev, c_st, n_st, m_st, *, heads):
    L = CHUNK
    d = MLSTM_DIM

    @pl.when(pl.program_id(1) == 0)
    def _():
        xprev[...] = jnp.zeros_like(xprev)
        c_st[...] = jnp.zeros_like(c_st)
        n_st[...] = jnp.zeros_like(n_st)
        m_st[...] = jnp.zeros_like(m_st)

    x = jnp.concatenate([mq_ref[...], mk_ref[...]], axis=1).astype(F32)
    y = _causal_conv(x, xprev[...], cw_ref[...], cb_ref[...])
    xprev[...] = x[L - SUBLANES:L, :]
    a = y * jax.nn.sigmoid(y)
    d_m = heads * d
    q = a[:, :d_m]
    k = a[:, d_m:] * (d ** -0.5)

    gt = gates_ref[...] + gb_ref[...]
    lane = lax.broadcasted_iota(jnp.int32, gt.shape, 1)
    gate_tile = jnp.where(lane < heads, gt, jnp.where(lane < 2 * heads, jax.nn.log_sigmoid(gt), 0.0))
    r = lax.broadcasted_iota(jnp.int32, (L, L), 0)
    cidx = lax.broadcasted_iota(jnp.int32, (L, L), 1)
    causal = r >= cidx
    cum_col = jnp.dot(causal.astype(F32), gate_tile, precision=lax.Precision.HIGHEST,
                      preferred_element_type=F32)
    gate_rows = gate_tile.T
    cum_row = jnp.dot(gate_rows, (r <= cidx).astype(F32), precision=lax.Precision.HIGHEST,
                      preferred_element_type=F32)

    n_all = n_st[...]
    m_all = m_st[...]
    H = range(heads)
    hs = [slice(h * d, (h + 1) * d) for h in H]
    cum_c = [cum_col[:, heads + h:heads + h + 1] for h in H]
    cum_r = [cum_row[heads + h:heads + h + 1, :] for h in H]
    ii_r = [gate_rows[h:h + 1, :] for h in H]
    ii_c = [gate_tile[:, h:h + 1] for h in H]
    m_prev = [m_all[h:h + 1, 0:1] for h in H]
    n_prev = [n_all[h:h + 1, :] for h in H]
    c_prev = [c_st[h] for h in H]
    qh = [q[:, hs[h]] for h in H]
    kh = [k[:, hs[h]] for h in H]
    qb = [qh[h].astype(BF16) for h in H]
    vb = [v_ref[:, hs[h]].astype(BF16) for h in H]
    logw = [jnp.where(causal, cum_c[h] - cum_r[h] + ii_r[h], -jnp.inf) for h in H]
    inter = [cum_c[h] + m_prev[h] for h in H]
    m_t = [jnp.maximum(inter[h], jnp.max(logw[h], axis=-1, keepdims=True)) for h in H]
    w = [jnp.exp(logw[h] - m_t[h]) for h in H]
    sc = [jnp.exp(inter[h] - m_t[h]) for h in H]
    qk = [_dot_nt(qb[h], kh[h].astype(BF16)) * w[h] for h in H]
    num = [_dot(qk[h].astype(BF16), vb[h]) + sc[h] * _dot(qb[h], c_prev[h].astype(BF16)) for h in H]
    den = [jnp.sum(qk[h], axis=-1, keepdims=True)
           + sc[h] * jnp.sum(qh[h] * n_prev[h], axis=-1, keepdims=True) for h in H]
    hv = [num[h] / jnp.maximum(jnp.abs(den[h]), jnp.exp(-m_t[h])) for h in H]
    last = [cum_c[h][L - 1:L, :] for h in H]
    logw_end = [last[h] - cum_c[h] + ii_c[h] for h in H]
    m_new = [jnp.maximum(last[h] + m_prev[h], jnp.max(logw_end[h], axis=0, keepdims=True)) for h in H]
    kw = [kh[h] * jnp.exp(logw_end[h] - m_new[h]) for h in H]
    decay = [jnp.exp(last[h] + m_prev[h] - m_new[h]) for h in H]
    for h in H:
        c_st[h] = decay[h] * c_prev[h] + _dot_tn(kw[h].astype(BF16), vb[h])
    n_st[0:heads, :] = jnp.concatenate(
        [decay[h] * n_prev[h] + jnp.sum(kw[h], axis=0, keepdims=True) for h in H], axis=0)
    m_st[0:heads, :] = jnp.concatenate(
        [jnp.broadcast_to(m_new[h], (1, m_st.shape[1])) for h in H], axis=0)
    yn = _group_norm_heads(hv, [gnw_ref[:, hs[h]] for h in H])
    for h in H:
        o_ref[:, hs[h]] = (jax.nn.sigmoid(og_ref[:, hs[h]].astype(F32)) * yn[h]).astype(o_ref.dtype)


def _mlstm(proj, tail, cols, conv_w, conv_b, gate_b, gn_w, *, batch, t):
    n = proj.shape[0]
    d_m = gn_w.shape[0]
    heads = d_m // MLSTM_DIM
    nc = t // CHUNK
    taps = conv_w.shape[0]
    gb = jnp.zeros((1, LANES), F32).at[0, :2 * heads].set(gate_b)
    row = lambda b, c: b * nc + c
    const2 = lambda b, c: (0, 0)
    return pl.pallas_call(
        functools.partial(_mlstm_kernel, heads=heads),
        grid=(batch, nc),
        in_specs=[
            pl.BlockSpec((CHUNK, d_m), lambda b, c: (row(b, c), cols["m_q"] // d_m)),
            pl.BlockSpec((CHUNK, d_m), lambda b, c: (row(b, c), cols["m_k"] // d_m)),
            pl.BlockSpec((CHUNK, d_m), lambda b, c: (row(b, c), cols["m_v"] // d_m)),
            pl.BlockSpec((CHUNK, d_m), lambda b, c: (row(b, c), cols["m_o"] // d_m)),
            pl.BlockSpec((CHUNK, LANES), lambda b, c: (row(b, c), cols["t_gates"] // LANES)),
            pl.BlockSpec((taps, 2 * d_m), const2),
            pl.BlockSpec((1, 2 * d_m), const2),
            pl.BlockSpec((1, LANES), const2),
            pl.BlockSpec((1, d_m), const2),
        ],
        out_specs=pl.BlockSpec((CHUNK, d_m), lambda b, c: (row(b, c), 0)),
        out_shape=jax.ShapeDtypeStruct((n, d_m), BF16),
        scratch_shapes=[
            pltpu.VMEM((SUBLANES, 2 * d_m), F32),
            pltpu.VMEM((heads, MLSTM_DIM, MLSTM_DIM), F32),
            pltpu.VMEM((SUBLANES, MLSTM_DIM), F32),
            pltpu.VMEM((SUBLANES, LANES), F32),
        ],
        compiler_params=_params(2),
        name="mlstm",
    )(proj, proj, proj, proj, tail, conv_w, conv_b.reshape(1, -1), gb, gn_w.reshape(1, d_m))


def _s5_kernel(u_ref, bd_ref, w_ref, v_ref, a_ref, y_ref, t_scr, e_ref, s_ref, *, seqs, nsub):
    s = S5_SUB

    @pl.when(pl.program_id(1) == 0)
    def _():
        zero = jnp.zeros((LANES, LANES), t_scr.dtype)
        for ti in range(s):
            for to in range(s):
                t_scr[ti * LANES:(ti + 1) * LANES, to * LANES:(to + 1) * LANES] = (
                    bd_ref[to - ti] if to >= ti else zero)

    rows = seqs * nsub
    u = jnp.concatenate([u_ref[pl.ds(ti, rows, stride=s), :] for ti in range(s)], axis=1).astype(BF16)
    e_ref[...] = _dot(u, w_ref[...])
    ns = a_ref.shape[1] // 2
    ar = a_ref[:, :ns]
    ai = a_ref[:, ns:]

    def body(c, carry):
        nxt = []
        for b in range(seqs):
            re, im = carry[b]
            row = pl.ds(b * nsub + c, 1)
            s_ref[row, :ns] = re
            s_ref[row, ns:] = im
            e = e_ref[row, :]
            nxt.append((ar * re - ai * im + e[:, :ns], ar * im + ai * re + e[:, ns:]))
        return tuple(nxt)

    zero = jnp.zeros((1, ns), F32)
    lax.fori_loop(0, nsub, body, tuple((zero, zero) for _ in range(seqs)), unroll=8)
    y = _dot(u, t_scr[...]) + _dot(s_ref[...].astype(BF16), v_ref[...])
    for to in range(s):
        y_ref[pl.ds(to, rows, stride=s), :] = y[:, to * LANES:(to + 1) * LANES]


def _s5_weights(a_re, a_im, log_step, b_re, b_im, c_re, c_im):
    hp = lax.Precision.HIGHEST
    g, p = a_re.shape
    ch = b_re.shape[-1]
    s = S5_SUB
    ngl = LANES // ch
    nq = g // ngl
    ar, ai = a_re.astype(F32), a_im.astype(F32)
    step = jnp.exp(log_step.astype(F32))[:, None]
    mag = jnp.exp(ar * step)
    abar_re, abar_im = mag * jnp.cos(ai * step), mag * jnp.sin(ai * step)
    xr, xi = abar_re - 1.0, abar_im
    den = ar * ar + ai * ai
    fr, fi = (xr * ar + xi * ai) / den, (xi * ar - xr * ai) / den
    br, bi = b_re.astype(F32), b_im.astype(F32)
    bbar_re = fr[..., None] * br - fi[..., None] * bi
    bbar_im = fr[..., None] * bi + fi[..., None] * br
    j = jnp.arange(s + 1, dtype=F32)[:, None, None]
    pmag = jnp.exp(ar[None] * step[None] * j)
    pr, pi = pmag * jnp.cos(ai[None] * step[None] * j), pmag * jnp.sin(ai[None] * step[None] * j)
    cr, ci = c_re.astype(F32), c_im.astype(F32)
    cp_re = cr[None] * pr[:, :, None, :] - ci[None] * pi[:, :, None, :]
    cp_im = cr[None] * pi[:, :, None, :] + ci[None] * pr[:, :, None, :]
    kern = (jnp.einsum('jgcp,gpd->jgcd', cp_re, bbar_re, precision=hp)
            - jnp.einsum('jgcp,gpd->jgcd', cp_im, bbar_im, precision=hp))
    eye = jnp.eye(ngl, dtype=F32)
    bd = kern.reshape(s + 1, nq, ngl, ch, ch).transpose(1, 0, 2, 4, 3)
    bd = bd[:, :, :, :, None, :] * eye[None, None, :, None, :, None]
    bd = bd.reshape(nq, s + 1, LANES, LANES)
    tt = jnp.arange(s)
    prr, pir = pr[s - 1 - tt], pi[s - 1 - tt]
    w_re = prr[..., None] * bbar_re[None] - pir[..., None] * bbar_im[None]
    w_im = prr[..., None] * bbar_im[None] + pir[..., None] * bbar_re[None]

    def w_expand(w):
        w = w.reshape(s, nq, ngl, p, ch).transpose(1, 0, 2, 4, 3)
        return w[:, :, :, :, None, :] * eye[None, None, :, None, :, None]

    w_blk = jnp.stack([w_expand(w_re), w_expand(w_im)], axis=4)
    w_blk = w_blk.reshape(nq, s * LANES, 2 * ngl * p)

    def v_expand(v):
        v = v.reshape(s, nq, ngl, ch, p).transpose(1, 2, 4, 0, 3)
        return v[:, :, :, :, None, :] * eye[None, :, None, None, :, None]

    v_blk = jnp.stack([v_expand(cp_re[1:]), v_expand(-cp_im[1:])], axis=1)
    v_blk = v_blk.reshape(nq, 2 * ngl * p, s * LANES)
    a_blk = jnp.concatenate([pr[s].reshape(nq, 1, ngl * p), pi[s].reshape(nq, 1, ngl * p)], axis=-1)
    return bd.astype(BF16), w_blk.astype(BF16), v_blk.astype(BF16), a_blk


def _s5(tail, cols, a_re, a_im, log_step, b_re, b_im, c_re, c_im, *, batch, t, seqs):
    n = tail.shape[0]
    s = S5_SUB
    nsub = t // s
    bd, w_blk, v_blk, a_blk = _s5_weights(a_re, a_im, log_step, b_re, b_im, c_re, c_im)
    nq = bd.shape[0]
    kw = s * LANES
    ns2 = w_blk.shape[-1]
    rows = seqs * nsub
    c0 = cols["t_su"] // LANES
    once = dict(pipeline_mode=pl.Buffered(1))
    return pl.pallas_call(
        functools.partial(_s5_kernel, seqs=seqs, nsub=nsub),
        grid=(nq, batch // seqs),
        in_specs=[
            pl.BlockSpec((seqs * t, LANES), lambda q, i: (i, c0 + q)),
            pl.BlockSpec((None, s + 1, LANES, LANES), lambda q, i: (q, 0, 0, 0)),
            pl.BlockSpec((None, kw, ns2), lambda q, i: (q, 0, 0), **once),
            pl.BlockSpec((None, ns2, kw), lambda q, i: (q, 0, 0), **once),
            pl.BlockSpec((None, 1, ns2), lambda q, i: (q, 0, 0)),
        ],
        out_specs=pl.BlockSpec((seqs * t, LANES), lambda q, i: (i, q)),
        out_shape=jax.ShapeDtypeStruct((n, nq * LANES), F32),
        scratch_shapes=[pltpu.VMEM((kw, kw), BF16), pltpu.VMEM((rows, ns2), F32),
                        pltpu.VMEM((rows, ns2), F32)],
        compiler_params=_params(2),
        name="s5_scan",
    )(tail, bd, w_blk, v_blk, a_blk)


def _s5_glu_kernel(y_ref, u_ref, d_ref, w_ref, b_ref, o_ref):
    y = y_ref[...]
    y = y + d_ref[...] * u_ref[...]
    g = jax.nn.gelu(y)
    z = _dot(g.astype(BF16), w_ref[...]) + b_ref[...]
    o_ref[...] = (g * jax.nn.sigmoid(z)).astype(o_ref.dtype)


def _s5_glu(y, tail, cols, d_skip, glu_w_all, glu_b, layer, *, tm):
    n, d_s5 = y.shape
    const2 = lambda i: (0, 0)
    return pl.pallas_call(
        _s5_glu_kernel,
        grid=(n // tm,),
        in_specs=[
            pl.BlockSpec((tm, d_s5), lambda i: (i, 0)),
            pl.BlockSpec((tm, d_s5), lambda i: (i, cols["t_su"] // d_s5)),
            pl.BlockSpec((1, d_s5), const2),
            pl.BlockSpec((None, d_s5, d_s5), lambda i: (layer, 0, 0)),
            pl.BlockSpec((1, d_s5), const2),
        ],
        out_specs=pl.BlockSpec((tm, d_s5), lambda i: (i, 0)),
        out_shape=jax.ShapeDtypeStruct((n, d_s5), BF16),
        compiler_params=_params(1),
        name="s5_glu",
    )(y, tail, d_skip.reshape(1, d_s5), glu_w_all, glu_b.reshape(1, d_s5))


def _out_proj_kernel(x_ref, a_ref, b_ref, c_ref, wa_ref, wb_ref, wc_ref, o_ref):
    acc = _dot(a_ref[...], wa_ref[...])
    acc = acc + _dot(b_ref[...], wb_ref[...])
    acc = acc + _dot(c_ref[...], wc_ref[...])
    o_ref[...] = x_ref[...] + acc


def _out_proj(x, ret, ml, s5o, w_out_all, layer, *, tm):
    n, d = x.shape
    da, db, dc = ret.shape[1], ml.shape[1], s5o.shape[1]
    assert da == db and (da + db) % dc == 0
    return pl.pallas_call(
        _out_proj_kernel,
        grid=(n // tm,),
        in_specs=[
            pl.BlockSpec((tm, d), lambda i: (i, 0)),
            pl.BlockSpec((tm, da), lambda i: (i, 0)),
            pl.BlockSpec((tm, db), lambda i: (i, 0)),
            pl.BlockSpec((tm, dc), lambda i: (i, 0)),
            pl.BlockSpec((None, da, d), lambda i: (layer, 0, 0)),
            pl.BlockSpec((None, db, d), lambda i: (layer, 1, 0)),
            pl.BlockSpec((None, dc, d), lambda i: (layer, (da + db) // dc, 0)),
        ],
        out_specs=pl.BlockSpec((tm, d), lambda i: (i, 0)),
        out_shape=jax.ShapeDtypeStruct((n, d), F32),
        compiler_params=_params(1),
        name="out_proj",
    )(x, ret, ml, s5o, w_out_all, w_out_all, w_out_all)


FFN_CHUNKS = 2


def _ffn_kernel(x_ref, nw_ref, wv_ref, wg_ref, cw_ref, cb_ref, wd_ref, fnw_ref,
                o_ref, h_ref, raw0, raw1, act0, act1, halo, *, final_norm, nj):
    ti = pl.program_id(1)
    j = pl.program_id(2)
    tm = x_ref.shape[0]
    tf = wd_ref.shape[0]
    wc = tf // FFN_CHUNKS

    @pl.when(j == 0)
    def _():
        x = x_ref[...]
        h_ref[...] = _rms(x, nw_ref[...]).astype(h_ref.dtype)
        o_ref[...] = x

    @pl.when(jnp.logical_and(j == 0, ti == 0))
    def _():
        halo[...] = jnp.zeros_like(halo)

    def up_chunk(raw_ref, c):
        h = h_ref[...]
        cs = slice(c * wc, (c + 1) * wc)
        raw_ref[:, :wc] = _dot(h, wv_ref[:, cs])
        raw_ref[:, wc:] = _dot(h, wg_ref[:, cs])

    def gate_chunk(raw_ref, act_ref, jj, c):
        def conv(up, k):
            prev = halo[k]
            halo[k] = up[tm - SUBLANES:tm, :]
            return _causal_conv(up, prev, cw_ref[k], cb_ref[k])

        val = conv(raw_ref[:, :wc], jj * FFN_CHUNKS + c)
        gate = conv(raw_ref[:, wc:], (nj + jj) * FFN_CHUNKS + c)
        act_ref[:, c * wc:(c + 1) * wc] = (gate * jax.nn.sigmoid(gate) * val).astype(act_ref.dtype)

    def down_phase(act_ref):
        o_ref[...] += _dot(act_ref[...], wd_ref[...])

    acts = (act0, act1)

    @pl.when(j == 0)
    def _():
        up_chunk(raw0, 0)
        up_chunk(raw1, 1)
        gate_chunk(raw0, acts[0], j, 0)

    for parity in (0, 1):
        @pl.when(jnp.logical_and(j % 2 == parity, jnp.logical_and(j >= 1, j < nj)))
        def _():
            gate_chunk(raw1, acts[1 - parity], j - 1, 1)
            up_chunk(raw0, 0)
            up_chunk(raw1, 1)
            down_phase(acts[1 - parity])
            gate_chunk(raw0, acts[parity], j, 0)

    @pl.when(j == nj)
    def _():
        gate_chunk(raw1, acts[1 - nj % 2], j - 1, 1)
        down_phase(acts[1 - nj % 2])
        if final_norm:
            o_ref[...] = _rms(o_ref[...], fnw_ref[...])


def _ffn(x, norm_w, w_up_all, conv_w, conv_b, w_down_all, final_w, layer, *, batch, t, tm, tf,
         final_norm):
    n, d = x.shape
    d_ff = w_down_all.shape[1]
    nj = d_ff // tf
    nt = t // tm
    taps = conv_w.shape[0]
    row = lambda b, i, j: (b * nt + i, 0)
    const2 = lambda b, i, j: (0, 0)
    up_j = lambda j: jnp.minimum(j, nj - 1)
    dn_j = lambda j: jnp.maximum(j - 1, 0)
    nk = 2 * nj * FFN_CHUNKS
    wc = tf // FFN_CHUNKS
    cw = conv_w.reshape(taps, nk, wc).transpose(1, 0, 2)
    cb = conv_b.reshape(nk, 1, wc)
    const3 = lambda b, i, j: (0, 0, 0)
    return pl.pallas_call(
        functools.partial(_ffn_kernel, final_norm=final_norm, nj=nj),
        grid=(batch, nt, nj + 1),
        in_specs=[
            pl.BlockSpec((tm, d), row, pipeline_mode=pl.Buffered(1)),
            pl.BlockSpec((1, d), const2),
            pl.BlockSpec((None, None, d, tf), lambda b, i, j: (layer, up_j(j), 0, 0)),
            pl.BlockSpec((None, None, d, tf), lambda b, i, j: (layer, up_j(j) + nj, 0, 0)),
            pl.BlockSpec((nk, taps, wc), const3),
            pl.BlockSpec((nk, 1, wc), const3),
            pl.BlockSpec((None, tf, d), lambda b, i, j: (layer, dn_j(j), 0)),
            pl.BlockSpec((1, d), const2),
        ],
        out_specs=pl.BlockSpec((tm, d), row),
        out_shape=jax.ShapeDtypeStruct((n, d), F32),
        scratch_shapes=[
            pltpu.VMEM((tm, d), BF16),
            pltpu.VMEM((tm, 2 * wc), F32),
            pltpu.VMEM((tm, 2 * wc), F32),
            pltpu.VMEM((tm, tf), BF16),
            pltpu.VMEM((tm, tf), BF16),
            pltpu.VMEM((nk, SUBLANES, wc), F32),
        ],
        compiler_params=_params(3),
        name="conv_ffn",
    )(x, norm_w.reshape(1, d), w_up_all, w_up_all, cw, cb, w_down_all, final_w.reshape(1, d))


def _pick(n, pref):
    return pref if n % pref == 0 else n


def kernel(x, norm1_w, w_in, mlstm_conv_w, mlstm_conv_b, mlstm_gate_b, ret_gn_w, mlstm_gn_w,
           s5_A_re, s5_A_im, s5_log_step, s5_B_re, s5_B_im, s5_C_re, s5_C_im, s5_D,
           s5_glu_w, s5_glu_b, w_out, norm2_w, ffn_w_up, ffn_conv_w, ffn_conv_b, ffn_w_down,
           final_norm_w):
    batch, t, d = x.shape
    depth = w_in.shape[0]
    n = batch * t
    d_ret, d_m, d_s5 = ret_gn_w.shape[1], mlstm_gn_w.shape[1], s5_D.shape[1]
    heads_r, heads_m = d_ret // RET_V_DIM, d_m // MLSTM_DIM
    d_qk = heads_r * RET_QK_DIM
    cols, off = {}, 0
    for name, size in (("r_q", d_qk), ("r_k", d_qk), ("r_v", d_ret), ("r_g", d_ret), ("m_q", d_m),
                       ("m_k", d_m), ("m_v", d_m), ("m_o", d_m), ("m_gates", 2 * heads_m), ("s_u", d_s5)):
        cols[name] = off
        off += size
    main = cols["m_gates"]
    cols["t_su"], cols["t_gates"] = 0, d_s5
    tn = main // 6
    assert main % LANES == 0 and tn % LANES == 0 and 2 * heads_m <= LANES
    tm = _pick(t, 512)
    tf = _pick(ffn_w_down.shape[1], 512)

    w_in_b = w_in.astype(BF16)
    w_out_b = w_out.astype(BF16)
    w_up_b = ffn_w_up.astype(BF16).reshape(depth, d, -1, tf).transpose(0, 2, 1, 3)
    w_down_b = ffn_w_down.astype(BF16)
    glu_w_b = s5_glu_w.astype(BF16)

    xf = x.reshape(n, d)
    for l in range(depth):
        w_tail = jnp.concatenate(
            [w_in[l][:, cols["s_u"]:cols["s_u"] + d_s5], w_in[l][:, main:main + 2 * heads_m],
             jnp.zeros((d, LANES - 2 * heads_m), F32)], axis=1).astype(BF16)
        proj, tail = _in_proj(xf, norm1_w[l], w_in_b, w_tail, l, main=main, tm=_pick(t, 1024), tn=tn)
        ret = _retention(proj, cols, ret_gn_w[l], batch=batch, t=t)
        ml = _mlstm(proj, tail, cols, mlstm_conv_w[l], mlstm_conv_b[l], mlstm_gate_b[l], mlstm_gn_w[l],
                    batch=batch, t=t)
        y5 = _s5(tail, cols, s5_A_re[l], s5_A_im[l], s5_log_step[l], s5_B_re[l], s5_B_im[l],
                 s5_C_re[l], s5_C_im[l], batch=batch, t=t, seqs=2 if batch % 2 == 0 else 1)
        s5o = _s5_glu(y5, tail, cols, s5_D[l], glu_w_b, s5_glu_b[l], l, tm=tm)
        xf = _out_proj(xf, ret, ml, s5o, w_out_b, l, tm=_pick(t, 256))
        xf = _ffn(xf, norm2_w[l], w_up_b, ffn_conv_w[l], ffn_conv_b[l], w_down_b, final_norm_w, l,
                  batch=batch, t=t, tm=_pick(t, 1024), tf=tf, final_norm=(l == depth - 1))
    return xf.reshape(batch, t, d)
```

```python
import functools

import jax
import jax.numpy as jnp
from jax import lax
from jax.experimental import pallas as pl
from jax.experimental.pallas import tpu as pltpu

RET_V_DIM = 128
RET_QK_DIM = 64
MLSTM_DIM = 128
ROPE_BASE = 10000.0
EPS = 1e-6

CHUNK = 128
MIXER_ROWS = 2 * CHUNK
S5_SUB = 16
SUBLANES = 8
LANES = 128

VMEM_LIMIT_BYTES = 56 * 1024 * 1024

F32 = jnp.float32
BF16 = jnp.bfloat16


def _params(n_grid):
    return pltpu.CompilerParams(
        dimension_semantics=("arbitrary",) * n_grid, vmem_limit_bytes=VMEM_LIMIT_BYTES)


def _rms(x, w):
    return x * lax.rsqrt(jnp.mean(x * x, axis=-1, keepdims=True) + EPS) * w


def _group_norm_heads(os, ws):
    mus = [jnp.mean(o, axis=-1, keepdims=True) for o in os]
    ds = [o - mu for o, mu in zip(os, mus)]
    vs = [jnp.mean(d * d, axis=-1, keepdims=True) for d in ds]
    return [d * lax.rsqrt(v + EPS) * w for d, v, w in zip(ds, vs, ws)]


def _dot(a, b):
    return jnp.dot(a, b, preferred_element_type=F32)


def _dot_nt(a, b):
    return lax.dot_general(a, b, (((1,), (1,)), ((), ())), preferred_element_type=F32)


def _dot_tn(a, b):
    return lax.dot_general(a, b, (((0,), (0,)), ((), ())), preferred_element_type=F32)


def _causal_conv(x, prev, cw, cb):
    taps = cw.shape[0]
    y = cb + cw[taps - 1:taps, :] * x
    row = lax.broadcasted_iota(jnp.int32, prev.shape, 0)
    for s in range(1, taps):
        sh = pltpu.roll(x, s, 0)
        head = jnp.where(row < s, pltpu.roll(prev, s, 0), sh[:SUBLANES, :])
        sh = jnp.concatenate([head, sh[SUBLANES:, :]], axis=0)
        y = y + cw[taps - 1 - s:taps - s, :] * sh
    return y


def _in_proj_kernel(x_ref, nw_ref, w_ref, wt_ref, o_ref, t_ref, h_ref):
    j = pl.program_id(1)
    last = pl.num_programs(1) - 1

    @pl.when(j == 0)
    def _():
        h_ref[...] = _rms(x_ref[...], nw_ref[...]).astype(h_ref.dtype)

    @pl.when(j < last)
    def _():
        o_ref[...] = _dot(h_ref[...], w_ref[...]).astype(o_ref.dtype)

    @pl.when(j == last)
    def _():
        t_ref[...] = _dot(h_ref[...], wt_ref[...])


def _in_proj(x, norm_w, w_all, w_tail, layer, *, main, tm, tn):
    n, d = x.shape
    nj = main // tn
    wt = w_tail.shape[1]
    return pl.pallas_call(
        _in_proj_kernel,
        grid=(n // tm, nj + 1),
        in_specs=[
            pl.BlockSpec((tm, d), lambda i, j: (i, 0)),
            pl.BlockSpec((1, d), lambda i, j: (0, 0)),
            pl.BlockSpec((None, d, tn), lambda i, j: (layer, 0, jnp.minimum(j, nj - 1))),
            pl.BlockSpec((d, wt), lambda i, j: (0, 0)),
        ],
        out_specs=[
            pl.BlockSpec((tm, tn), lambda i, j: (i, jnp.minimum(j, nj - 1))),
            pl.BlockSpec((tm, wt), lambda i, j: (i, 0)),
        ],
        out_shape=[
            jax.ShapeDtypeStruct((n, main), BF16),
            jax.ShapeDtypeStruct((n, wt), F32),
        ],
        scratch_shapes=[pltpu.VMEM((tm, d), BF16)],
        compiler_params=_params(2),
        name="norm_in_proj",
    )(x, norm_w.reshape(1, d), w_all, w_tail)


def _retention_kernel(q_ref, k_ref, v_ref, g_ref, cos_ref, sin_ref, intra_ref, qd_ref, kd_ref,
                      cd_ref, gnw_ref, o_ref, st_ref, *, heads):
    @pl.when(pl.program_id(1) == 0)
    def _():
        st_ref[...] = jnp.zeros_like(st_ref)

    width = cos_ref.shape[1]
    half = RET_QK_DIM // 2
    first_half = (lax.broadcasted_iota(jnp.int32, (CHUNK, width), 1) % RET_QK_DIM) < half
    H = range(heads)
    qs = [slice(h * RET_QK_DIM, (h + 1) * RET_QK_DIM) for h in H]
    vs = [slice(h * RET_V_DIM, (h + 1) * RET_V_DIM) for h in H]

    for r0 in range(0, q_ref.shape[0], CHUNK):
        rows = slice(r0, r0 + CHUNK)
        cos = cos_ref[rows, :]
        sin = sin_ref[rows, :]

        def rope(x):
            swapped = jnp.where(first_half, pltpu.roll(x, width - half, 1), pltpu.roll(x, half, 1))
            return x * cos + swapped * sin

        q = rope(q_ref[rows, :].astype(F32)) * (RET_QK_DIM ** -0.5)
        k = rope(k_ref[rows, :].astype(F32))
        qh = [q[:, qs[h]].astype(BF16) for h in H]
        kh = [k[:, qs[h]] for h in H]
        vh = [v_ref[rows, vs[h]].astype(BF16) for h in H]
        st = [st_ref[h] for h in H]
        s = [_dot_nt(qh[h], kh[h].astype(BF16)) * intra_ref[h] for h in H]
        cross = [_dot(qh[h], st[h].astype(BF16)) * qd_ref[h] for h in H]
        inner = [_dot(s[h].astype(BF16), vh[h]) for h in H]
        for h in H:
            st_ref[h] = st[h] * cd_ref[h] + _dot_tn((kh[h] * kd_ref[h]).astype(BF16), vh[h])
        y = _group_norm_heads([inner[h] + cross[h] for h in H], [gnw_ref[:, vs[h]] for h in H])
        for h in H:
            g = g_ref[rows, vs[h]].astype(F32)
            o_ref[rows, vs[h]] = (g * jax.nn.sigmoid(g) * y[h]).astype(o_ref.dtype)


def _retention_tables(t, heads):
    dh = RET_QK_DIM
    inv = ROPE_BASE ** (-jnp.arange(0, dh, 2, dtype=F32) / dh)
    ang = jnp.arange(t, dtype=F32)[:, None] * inv[None, :]
    cos, sin = jnp.cos(ang), jnp.sin(ang)
    cos_t = jnp.tile(jnp.concatenate([cos, cos], axis=-1), (1, heads))
    sin_t = jnp.tile(jnp.concatenate([-sin, sin], axis=-1), (1, heads))
    log_gamma = jnp.log1p(-jnp.exp2(-5.0 - jnp.arange(heads, dtype=F32)))
    idx = jnp.arange(CHUNK, dtype=F32)
    diff = idx[:, None] - idx[None, :]
    intra = jnp.where(diff[None] >= 0,
                      jnp.exp(jnp.maximum(diff, 0.0)[None] * log_gamma[:, None, None]), 0.0)
    q_decay = jnp.exp((idx[None, :] + 1.0) * log_gamma[:, None])
    k_decay = jnp.exp((CHUNK - 1.0 - idx)[None, :] * log_gamma[:, None])
    chunk_decay = jnp.exp(CHUNK * log_gamma)
    qd = jnp.broadcast_to(q_decay[:, :, None], (heads, CHUNK, RET_V_DIM))
    kd = jnp.broadcast_to(k_decay[:, :, None], (heads, CHUNK, RET_QK_DIM))
    cd = jnp.broadcast_to(chunk_decay[:, None, None], (heads, RET_QK_DIM, RET_V_DIM))
    return cos_t, sin_t, intra, qd, kd, cd


def _retention(proj, cols, gn_w, *, batch, t):
    n = proj.shape[0]
    d_ret = gn_w.shape[0]
    heads = d_ret // RET_V_DIM
    d_qk = heads * RET_QK_DIM
    rows = _pick(t, MIXER_ROWS)
    nc = t // rows
    cos_t, sin_t, intra, qd, kd, cd = _retention_tables(t, heads)
    row = lambda b, c: b * nc + c
    const3 = lambda b, c: (0, 0, 0)
    return pl.pallas_call(
        functools.partial(_retention_kernel, heads=heads),
        grid=(batch, nc),
        in_specs=[
            pl.BlockSpec((rows, d_qk), lambda b, c: (row(b, c), cols["r_q"] // d_qk)),
            pl.BlockSpec((rows, d_qk), lambda b, c: (row(b, c), cols["r_k"] // d_qk)),
            pl.BlockSpec((rows, d_ret), lambda b, c: (row(b, c), cols["r_v"] // d_ret)),
            pl.BlockSpec((rows, d_ret), lambda b, c: (row(b, c), cols["r_g"] // d_ret)),
            pl.BlockSpec((rows, d_qk), lambda b, c: (c, 0)),
            pl.BlockSpec((rows, d_qk), lambda b, c: (c, 0)),
            pl.BlockSpec(intra.shape, const3),
            pl.BlockSpec(qd.shape, const3),
            pl.BlockSpec(kd.shape, const3),
            pl.BlockSpec(cd.shape, const3),
            pl.BlockSpec((1, d_ret), lambda b, c: (0, 0)),
        ],
        out_specs=pl.BlockSpec((rows, d_ret), lambda b, c: (row(b, c), 0)),
        out_shape=jax.ShapeDtypeStruct((n, d_ret), BF16),
        scratch_shapes=[pltpu.VMEM((heads, RET_QK_DIM, RET_V_DIM), F32)],
        compiler_params=_params(2),
        name="retention",
    )(proj, proj, proj, proj, cos_t, sin_t, intra, qd, kd, cd, gn_w.reshape(1, d_ret))


def _mlstm_kernel(mq_ref, mk_ref, v_ref, og_ref, gates_ref, cw_ref, cb_ref, gb_ref, gnw_ref, o_ref,
                  xprev, c_st, n_st, m_st, *, heads):
    L = CHUNK
    d = MLSTM_DIM

    @pl.when(pl.program_id(1) == 0)
    def _():
        xprev[...] = jnp.zeros_like(xprev)
        c_st[...] = jnp.zeros_like(c_st)
        n_st[...] = jnp.zeros_like(n_st)
        m_st[...] = jnp.zeros_like(m_st)

    for r0 in range(0, mq_ref.shape[0], L):
        _mlstm_chunk(slice(r0, r0 + L), mq_ref, mk_ref, v_ref, og_ref, gates_ref, cw_ref, cb_ref, gb_ref,
                     gnw_ref, o_ref, xprev, c_st, n_st, m_st, heads)


def _mlstm_chunk(rows, mq_ref, mk_ref, v_ref, og_ref, gates_ref, cw_ref, cb_ref, gb_ref, gnw_ref, o_ref,
                 xprev, c_st, n_st, m_st, heads):
    L = CHUNK
    d = MLSTM_DIM
    x = jnp.concatenate([mq_ref[rows, :], mk_ref[rows, :]], axis=1).astype(F32)
    y = _causal_conv(x, xprev[...], cw_ref[...], cb_ref[...])
    xprev[...] = x[L - SUBLANES:L, :]
    a = y * jax.nn.sigmoid(y)
    d_m = heads * d
    q = a[:, :d_m]
    k = a[:, d_m:] * (d ** -0.5)

    gt = gates_ref[rows, :] + gb_ref[...]
    lane = lax.broadcasted_iota(jnp.int32, gt.shape, 1)
    gate_tile = jnp.where(lane < heads, gt, jnp.where(lane < 2 * heads, jax.nn.log_sigmoid(gt), 0.0))
    r = lax.broadcasted_iota(jnp.int32, (L, L), 0)
    cidx = lax.broadcasted_iota(jnp.int32, (L, L), 1)
    causal = r >= cidx
    cum_col = jnp.dot(causal.astype(F32), gate_tile, precision=lax.Precision.HIGHEST,
                      preferred_element_type=F32)
    gate_rows = gate_tile.T
    cum_row = jnp.dot(gate_rows, (r <= cidx).astype(F32), precision=lax.Precision.HIGHEST,
                      preferred_element_type=F32)

    n_all = n_st[...]
    m_all = m_st[...]
    H = range(heads)
    hs = [slice(h * d, (h + 1) * d) for h in H]
    cum_c = [cum_col[:, heads + h:heads + h + 1] for h in H]
    cum_r = [cum_row[heads + h:heads + h + 1, :] for h in H]
    ii_r = [gate_rows[h:h + 1, :] for h in H]
    ii_c = [gate_tile[:, h:h + 1] for h in H]
    m_prev = [m_all[h:h + 1, 0:1] for h in H]
    n_prev = [n_all[h:h + 1, :] for h in H]
    c_prev = [c_st[h] for h in H]
    qh = [q[:, hs[h]] for h in H]
    kh = [k[:, hs[h]] for h in H]
    qb = [qh[h].astype(BF16) for h in H]
    vb = [v_ref[rows, hs[h]].astype(BF16) for h in H]
    logw =[jnp.where(causal, cum_c[h] - cum_r[h] + ii_r[h], -jnp.inf) for h in H]
    inter = [cum_c[h] + m_prev[h] for h in H]
    m_t = [jnp.maximum(inter[h], jnp.max(logw[h], axis=-1, keepdims=True)) for h in H]
    w = [jnp.exp(logw[h] - m_t[h]) for h in H]
    sc = [jnp.exp(inter[h] - m_t[h]) for h in H]
    qk = [_dot_nt(qb[h], kh[h].astype(BF16)) * w[h] for h in H]
    num = [_dot(qk[h].astype(BF16), vb[h]) + sc[h] * _dot(qb[h], c_prev[h].astype(BF16)) for h in H]
    den = [jnp.sum(qk[h], axis=-1, keepdims=True)
           + sc[h] * jnp.sum(qh[h] * n_prev[h], axis=-1, keepdims=True) for h in H]
    hv = [num[h] / jnp.maximum(jnp.abs(den[h]), jnp.exp(-m_t[h])) for h in H]
    last = [cum_c[h][L - 1:L, :] for h in H]
    logw_end = [last[h] - cum_c[h] + ii_c[h] for h in H]
    m_new = [jnp.maximum(last[h] + m_prev[h], jnp.max(logw_end[h], axis=0, keepdims=True)) for h in H]
    kw = [kh[h] * jnp.exp(logw_end[h] - m_new[h]) for h in H]
    decay = [jnp.exp(last[h] + m_prev[h] - m_new[h]) for h in H]
    for h in H:
        c_st[h] = decay[h] * c_prev[h] + _dot_tn(kw[h].astype(BF16), vb[h])
    n_st[0:heads, :] = jnp.concatenate(
        [decay[h] * n_prev[h] + jnp.sum(kw[h], axis=0, keepdims=True) for h in H], axis=0)
    m_st[0:heads, :] = jnp.concatenate(
        [jnp.broadcast_to(m_new[h], (1, m_st.shape[1])) for h in H], axis=0)
    yn = _group_norm_heads(hv, [gnw_ref[:, hs[h]] for h in H])
    for h in H:
        o_ref[rows, hs[h]] = (jax.nn.sigmoid(og_ref[rows, hs[h]].astype(F32)) * yn[h]).astype(o_ref.dtype)


def _mlstm(proj, tail, cols, conv_w, conv_b, gate_b, gn_w, *, batch, t):
    n = proj.shape[0]
    d_m = gn_w.shape[0]
    heads = d_m // MLSTM_DIM
    rows = _pick(t, MIXER_ROWS)
    nc = t // rows
    taps = conv_w.shape[0]
    gb = jnp.zeros((1, LANES), F32).at[0, :2 * heads].set(gate_b)
    row = lambda b, c: b * nc + c
    const2 = lambda b, c: (0, 0)
    return pl.pallas_call(
        functools.partial(_mlstm_kernel, heads=heads),
        grid=(batch, nc),
        in_specs=[
            pl.BlockSpec((rows, d_m), lambda b, c: (row(b, c), cols["m_q"] // d_m)),
            pl.BlockSpec((rows, d_m), lambda b, c: (row(b, c), cols["m_k"] // d_m)),
            pl.BlockSpec((rows, d_m), lambda b, c: (row(b, c), cols["m_v"] // d_m)),
            pl.BlockSpec((rows, d_m), lambda b, c: (row(b, c), cols["m_o"] // d_m)),
            pl.BlockSpec((rows, LANES), lambda b, c: (row(b, c), cols["t_gates"] // LANES)),
            pl.BlockSpec((taps, 2 * d_m), const2),
            pl.BlockSpec((1, 2 * d_m), const2),
            pl.BlockSpec((1, LANES), const2),
            pl.BlockSpec((1, d_m), const2),
        ],
        out_specs=pl.BlockSpec((rows, d_m), lambda b, c: (row(b, c), 0)),
        out_shape=jax.ShapeDtypeStruct((n, d_m), BF16),
        scratch_shapes=[
            pltpu.VMEM((SUBLANES, 2 * d_m), F32),
            pltpu.VMEM((heads, MLSTM_DIM, MLSTM_DIM), F32),
            pltpu.VMEM((SUBLANES, MLSTM_DIM), F32),
            pltpu.VMEM((SUBLANES, LANES), F32),
        ],
        compiler_params=_params(2),
        name="mlstm",
    )(proj, proj, proj, proj, tail, conv_w, conv_b.reshape(1, -1), gb, gn_w.reshape(1, d_m))


def _s5_kernel(u_ref, bd_ref, w_ref, v_ref, a_ref, y_ref, t_scr, e_ref, s_ref, *, seqs, nsub):
    s = S5_SUB

    @pl.when(pl.program_id(1) == 0)
    def _():
        zero = jnp.zeros((LANES, LANES), t_scr.dtype)
        for ti in range(s):
            for to in range(s):
                t_scr[ti * LANES:(ti + 1) * LANES, to * LANES:(to + 1) * LANES] = (
                    bd_ref[to - ti] if to >= ti else zero)

    rows = seqs * nsub
    u = jnp.concatenate([u_ref[pl.ds(ti, rows, stride=s), :] for ti in range(s)], axis=1).astype(BF16)
    e_ref[...] = _dot(u, w_ref[...])
    ns = a_ref.shape[1] // 2
    ar = a_ref[:, :ns]
    ai = a_ref[:, ns:]

    def body(c, carry):
        nxt = []
        for b in range(seqs):
            re, im = carry[b]
            row = pl.ds(b * nsub + c, 1)
            s_ref[row, :ns] = re
            s_ref[row, ns:] = im
            e = e_ref[row, :]
            nxt.append((ar * re - ai * im + e[:, :ns], ar * im + ai * re + e[:, ns:]))
        return tuple(nxt)

    zero = jnp.zeros((1, ns), F32)
    lax.fori_loop(0, nsub, body, tuple((zero, zero) for _ in range(seqs)), unroll=8)
    y = _dot(u, t_scr[...]) + _dot(s_ref[...].astype(BF16), v_ref[...])
    for to in range(s):
        y_ref[pl.ds(to, rows, stride=s), :] = y[:, to * LANES:(to + 1) * LANES]


def _s5_weights(a_re, a_im, log_step, b_re, b_im, c_re, c_im):
    hp = lax.Precision.HIGHEST
    g, p = a_re.shape
    ch = b_re.shape[-1]
    s = S5_SUB
    ngl = LANES // ch
    nq = g // ngl
    ar, ai = a_re.astype(F32), a_im.astype(F32)
    step = jnp.exp(log_step.astype(F32))[:, None]
    mag = jnp.exp(ar * step)
    abar_re, abar_im = mag * jnp.cos(ai * step), mag * jnp.sin(ai * step)
    xr, xi = abar_re - 1.0, abar_im
    den = ar * ar + ai * ai
    fr, fi = (xr * ar + xi * ai) / den, (xi * ar - xr * ai) / den
    br, bi = b_re.astype(F32), b_im.astype(F32)
    bbar_re = fr[..., None] * br - fi[..., None] * bi
    bbar_im = fr[..., None] * bi + fi[..., None] * br
    j = jnp.arange(s + 1, dtype=F32)[:, None, None]
    pmag = jnp.exp(ar[None] * step[None] * j)
    pr, pi = pmag * jnp.cos(ai[None] * step[None] * j), pmag * jnp.sin(ai[None] * step[None] * j)
    cr, ci = c_re.astype(F32), c_im.astype(F32)
    cp_re = cr[None] * pr[:, :, None, :] - ci[None] * pi[:, :, None, :]
    cp_im = cr[None] * pi[:, :, None, :] + ci[None] * pr[:, :, None, :]
    kern = (jnp.einsum('jgcp,gpd->jgcd', cp_re, bbar_re, precision=hp)
            - jnp.einsum('jgcp,gpd->jgcd', cp_im, bbar_im, precision=hp))
    eye = jnp.eye(ngl, dtype=F32)
    bd = kern.reshape(s + 1, nq, ngl, ch, ch).transpose(1, 0, 2, 4, 3)
    bd = bd[:, :, :, :, None, :] * eye[None, None, :, None, :, None]
    bd = bd.reshape(nq, s + 1, LANES, LANES)
    tt = jnp.arange(s)
    prr, pir = pr[s - 1 - tt], pi[s - 1 - tt]
    w_re = prr[..., None] * bbar_re[None] - pir[..., None] * bbar_im[None]
    w_im = prr[..., None] * bbar_im[None] + pir[..., None] * bbar_re[None]

    def w_expand(w):
        w = w.reshape(s, nq, ngl, p, ch).transpose(1, 0, 2, 4, 3)
        return w[:, :, :, :, None, :] * eye[None, None, :, None, :, None]

    w_blk = jnp.stack([w_expand(w_re), w_expand(w_im)], axis=4)
    w_blk = w_blk.reshape(nq, s * LANES, 2 * ngl * p)

    def v_expand(v):
        v = v.reshape(s, nq, ngl, ch, p).transpose(1, 2, 4, 0, 3)
        return v[:, :, :, :, None, :] * eye[None, :, None, None, :, None]

    v_blk = jnp.stack([v_expand(cp_re[1:]), v_expand(-cp_im[1:])], axis=1)
    v_blk = v_blk.reshape(nq, 2 * ngl * p, s * LANES)
    a_blk = jnp.concatenate([pr[s].reshape(nq, 1, ngl * p), pi[s].reshape(nq, 1, ngl * p)], axis=-1)
    return bd.astype(BF16), w_blk.astype(BF16), v_blk.astype(BF16), a_blk


def _s5(tail, cols, a_re, a_im, log_step, b_re, b_im, c_re, c_im, *, batch, t, seqs):
    n = tail.shape[0]
    s = S5_SUB
    nsub = t // s
    bd, w_blk, v_blk, a_blk = _s5_weights(a_re, a_im, log_step, b_re, b_im, c_re, c_im)
    nq = bd.shape[0]
    kw = s * LANES
    ns2 = w_blk.shape[-1]
    rows = seqs * nsub
    c0 = cols["t_su"] // LANES
    once = dict(pipeline_mode=pl.Buffered(1))
    return pl.pallas_call(
        functools.partial(_s5_kernel, seqs=seqs, nsub=nsub),
        grid=(nq, batch // seqs),
        in_specs=[
            pl.BlockSpec((seqs * t, LANES), lambda q, i: (i, c0 + q)),
            pl.BlockSpec((None, s + 1, LANES, LANES), lambda q, i: (q, 0, 0, 0)),
            pl.BlockSpec((None, kw, ns2), lambda q, i: (q, 0, 0), **once),
            pl.BlockSpec((None, ns2, kw), lambda q, i: (q, 0, 0), **once),
            pl.BlockSpec((None, 1, ns2), lambda q, i: (q, 0, 0)),
        ],
        out_specs=pl.BlockSpec((seqs * t, LANES), lambda q, i: (i, q)),
        out_shape=jax.ShapeDtypeStruct((n, nq * LANES), F32),
        scratch_shapes=[pltpu.VMEM((kw, kw), BF16), pltpu.VMEM((rows, ns2), F32),
                        pltpu.VMEM((rows, ns2), F32)],
        compiler_params=_params(2),
        name="s5_scan",
    )(tail, bd, w_blk, v_blk, a_blk)


def _s5_glu_kernel(y_ref, u_ref, d_ref, w_ref, b_ref, o_ref):
    y = y_ref[...]
    y = y + d_ref[...] * u_ref[...]
    g = jax.nn.gelu(y)
    z = _dot(g.astype(BF16), w_ref[...]) + b_ref[...]
    o_ref[...] = (g * jax.nn.sigmoid(z)).astype(o_ref.dtype)


def _s5_glu(y, tail, cols, d_skip, glu_w_all, glu_b, layer, *, tm):
    n, d_s5 = y.shape
    const2 = lambda i: (0, 0)
    return pl.pallas_call(
        _s5_glu_kernel,
        grid=(n // tm,),
        in_specs=[
            pl.BlockSpec((tm, d_s5), lambda i: (i, 0)),
            pl.BlockSpec((tm, d_s5), lambda i: (i, cols["t_su"] // d_s5)),
            pl.BlockSpec((1, d_s5), const2),
            pl.BlockSpec((None, d_s5, d_s5), lambda i: (layer, 0, 0)),
            pl.BlockSpec((1, d_s5), const2),
        ],
        out_specs=pl.BlockSpec((tm, d_s5), lambda i: (i, 0)),
        out_shape=jax.ShapeDtypeStruct((n, d_s5), BF16),
        compiler_params=_params(1),
        name="s5_glu",
    )(y, tail, d_skip.reshape(1, d_s5), glu_w_all, glu_b.reshape(1, d_s5))


def _out_proj_kernel(x_ref, a_ref, b_ref, c_ref, wa_ref, wb_ref, wc_ref, o_ref):
    acc = _dot(a_ref[...], wa_ref[...])
    acc = acc + _dot(b_ref[...], wb_ref[...])
    acc = acc + _dot(c_ref[...], wc_ref[...])
    o_ref[...] = x_ref[...] + acc


def _out_proj(x, ret, ml, s5o, w_out_all, layer, *, tm):
    n, d = x.shape
    da, db, dc = ret.shape[1], ml.shape[1], s5o.shape[1]
    assert da == db and (da + db) % dc == 0
    return pl.pallas_call(
        _out_proj_kernel,
        grid=(n // tm,),
        in_specs=[
            pl.BlockSpec((tm, d), lambda i: (i, 0)),
            pl.BlockSpec((tm, da), lambda i: (i, 0)),
            pl.BlockSpec((tm, db), lambda i: (i, 0)),
            pl.BlockSpec((tm, dc), lambda i: (i, 0)),
            pl.BlockSpec((None, da, d), lambda i: (layer, 0, 0)),
            pl.BlockSpec((None, db, d), lambda i: (layer, 1, 0)),
            pl.BlockSpec((None, dc, d), lambda i: (layer, (da + db) // dc, 0)),
        ],
        out_specs=pl.BlockSpec((tm, d), lambda i: (i, 0)),
        out_shape=jax.ShapeDtypeStruct((n, d), F32),
        compiler_params=_params(1),
        name="out_proj",
    )(x, ret, ml, s5o, w_out_all, w_out_all, w_out_all)


FFN_CHUNKS = 2


def _ffn_kernel(x_ref, nw_ref, wv_ref, wg_ref, cw_ref, cb_ref, wd_ref, fnw_ref,
                o_ref, h_ref, raw0, raw1, act0, act1, halo, *, final_norm, nj):
    ti = pl.program_id(1)
    j = pl.program_id(2)
    tm = x_ref.shape[0]
    tf = wd_ref.shape[0]
    wc = tf // FFN_CHUNKS

    @pl.when(j == 0)
    def _():
        x = x_ref[...]
        h_ref[...] = _rms(x, nw_ref[...]).astype(h_ref.dtype)
        o_ref[...] = x

    @pl.when(jnp.logical_and(j == 0, ti == 0))
    def _():
        halo[...] = jnp.zeros_like(halo)

    def up_chunk(raw_ref, c):
        h = h_ref[...]
        cs = slice(c * wc, (c + 1) * wc)
        raw_ref[:, :wc] = _dot(h, wv_ref[:, cs])
        raw_ref[:, wc:] = _dot(h, wg_ref[:, cs])

    def gate_chunk(raw_ref, act_ref, jj, c):
        def conv(up, k):
            prev = halo[k]
            halo[k] = up[tm - SUBLANES:tm, :]
            return _causal_conv(up, prev, cw_ref[k], cb_ref[k])

        val = conv(raw_ref[:, :wc], jj * FFN_CHUNKS + c)
        gate = conv(raw_ref[:, wc:], (nj + jj) * FFN_CHUNKS + c)
        act_ref[:, c * wc:(c + 1) * wc] = (gate * jax.nn.sigmoid(gate) * val).astype(act_ref.dtype)

    def down_phase(act_ref):
        o_ref[...] += _dot(act_ref[...], wd_ref[...])

    acts = (act0, act1)

    @pl.when(j == 0)
    def _():
        up_chunk(raw0, 0)
        up_chunk(raw1, 1)
        gate_chunk(raw0, acts[0], j, 0)

    for parity in (0, 1):
        @pl.when(jnp.logical_and(j % 2 == parity, jnp.logical_and(j >= 1, j < nj)))
        def _():
            gate_chunk(raw1, acts[1 - parity], j - 1, 1)
            up_chunk(raw0, 0)
            up_chunk(raw1, 1)
            down_phase(acts[1 - parity])
            gate_chunk(raw0, acts[parity], j, 0)

    @pl.when(j == nj)
    def _():
        gate_chunk(raw1, acts[1 - nj % 2], j - 1, 1)
        down_phase(acts[1 - nj % 2])
        if final_norm:
            o_ref[...] = _rms(o_ref[...], fnw_ref[...])


def _ffn(x, norm_w, w_up_all, conv_w, conv_b, w_down_all, final_w, layer, *, batch, t, tm, tf,
         final_norm):
    n, d = x.shape
    d_ff = w_down_all.shape[1]
    nj = d_ff // tf
    nt = t // tm
    taps = conv_w.shape[0]
    row = lambda b, i, j: (b * nt + i, 0)
    const2 = lambda b, i, j: (0, 0)
    up_j = lambda j: jnp.minimum(j, nj - 1)
    dn_j = lambda j: jnp.maximum(j - 1, 0)
    nk = 2 * nj * FFN_CHUNKS
    wc = tf // FFN_CHUNKS
    cw = conv_w.reshape(taps, nk, wc).transpose(1, 0, 2)
    cb = conv_b.reshape(nk, 1, wc)
    const3 = lambda b, i, j: (0, 0, 0)
    return pl.pallas_call(
        functools.partial(_ffn_kernel, final_norm=final_norm, nj=nj),
        grid=(batch, nt, nj + 1),
        in_specs=[
            pl.BlockSpec((tm, d), row, pipeline_mode=pl.Buffered(1)),
            pl.BlockSpec((1, d), const2),
            pl.BlockSpec((None, None, d, tf), lambda b, i, j: (layer, up_j(j), 0, 0)),
            pl.BlockSpec((None, None, d, tf), lambda b, i, j: (layer, up_j(j) + nj, 0, 0)),
            pl.BlockSpec((nk, taps, wc), const3),
            pl.BlockSpec((nk, 1, wc), const3),
            pl.BlockSpec((None, tf, d), lambda b, i, j: (layer, dn_j(j), 0)),
            pl.BlockSpec((1, d), const2),
        ],
        out_specs=pl.BlockSpec((tm, d), row),
        out_shape=jax.ShapeDtypeStruct((n, d), F32),
        scratch_shapes=[
            pltpu.VMEM((tm, d), BF16),
            pltpu.VMEM((tm, 2 * wc), F32),
            pltpu.VMEM((tm, 2 * wc), F32),
            pltpu.VMEM((tm, tf), BF16),
            pltpu.VMEM((tm, tf), BF16),
            pltpu.VMEM((nk, SUBLANES, wc), F32),
        ],
        compiler_params=_params(3),
        name="conv_ffn",
    )(x, norm_w.reshape(1, d), w_up_all, w_up_all, cw, cb, w_down_all, final_w.reshape(1, d))


def _pick(n, pref):
    return pref if n % pref == 0 else n


def kernel(x, norm1_w, w_in, mlstm_conv_w, mlstm_conv_b, mlstm_gate_b, ret_gn_w, mlstm_gn_w,
           s5_A_re, s5_A_im, s5_log_step, s5_B_re, s5_B_im, s5_C_re, s5_C_im, s5_D,
           s5_glu_w, s5_glu_b, w_out, norm2_w, ffn_w_up, ffn_conv_w, ffn_conv_b, ffn_w_down,
           final_norm_w):
    batch, t, d = x.shape
    depth = w_in.shape[0]
    n = batch * t
    d_ret, d_m, d_s5 = ret_gn_w.shape[1], mlstm_gn_w.shape[1], s5_D.shape[1]
    heads_r, heads_m = d_ret // RET_V_DIM, d_m // MLSTM_DIM
    d_qk = heads_r * RET_QK_DIM
    cols, off = {}, 0
    for name, size in (("r_q", d_qk), ("r_k", d_qk), ("r_v", d_ret), ("r_g", d_ret), ("m_q", d_m),
                       ("m_k", d_m), ("m_v", d_m), ("m_o", d_m), ("m_gates", 2 * heads_m), ("s_u", d_s5)):
        cols[name] = off
        off += size
    main = cols["m_gates"]
    cols["t_su"], cols["t_gates"] = 0, d_s5
    tn = main // 6
    assert main % LANES == 0 and tn % LANES == 0 and 2 * heads_m <= LANES
    tm = _pick(t, 512)
    tf = _pick(ffn_w_down.shape[1], 512)

    w_in_b = w_in.astype(BF16)
    w_out_b = w_out.astype(BF16)
    w_up_b = ffn_w_up.astype(BF16).reshape(depth, d, -1, tf).transpose(0, 2, 1, 3)
    w_down_b = ffn_w_down.astype(BF16)
    glu_w_b = s5_glu_w.astype(BF16)

    xf = x.reshape(n, d)
    for l in range(depth):
        w_tail = jnp.concatenate(
            [w_in[l][:, cols["s_u"]:cols["s_u"] + d_s5], w_in[l][:, main:main + 2 * heads_m],
             jnp.zeros((d, LANES - 2 * heads_m), F32)], axis=1).astype(BF16)
        proj, tail = _in_proj(xf, norm1_w[l], w_in_b, w_tail, l, main=main, tm=_pick(t, 1024), tn=tn)
        ret = _retention(proj, cols, ret_gn_w[l], batch=batch, t=t)
        ml = _mlstm(proj, tail, cols, mlstm_conv_w[l], mlstm_conv_b[l], mlstm_gate_b[l], mlstm_gn_w[l],
                    batch=batch, t=t)
        y5 = _s5(tail, cols, s5_A_re[l], s5_A_im[l], s5_log_step[l], s5_B_re[l], s5_B_im[l],
                 s5_C_re[l], s5_C_im[l], batch=batch, t=t, seqs=2 if batch % 2 == 0 else 1)
        s5o = _s5_glu(y5, tail, cols, s5_D[l], glu_w_b, s5_glu_b[l], l, tm=tm)
        xf = _out_proj(xf, ret, ml, s5o, w_out_b, l, tm=_pick(t, 256))
        xf = _ffn(xf, norm2_w[l], w_up_b, ffn_conv_w[l], ffn_conv_b[l], w_down_b, final_norm_w, l,
                  batch=batch, t=t, tm=_pick(t, 1024), tf=tf, final_norm=(l == depth - 1))
    return xf.reshape(batch, t, d)
```
